```python
import math
import jax, jax.numpy as jnp
from jax import lax
import numpy as np

D_MODEL = 1024
BATCH = 4
SEQ = 8192
DEPTH = 2

GRID_W = 64
CTX_LEN = 256
HEAD_DIM = 64
ROPE_THETA = 10000.0
Q_BLOCK = 128
EPS = 1e-6
A_HEADS = D_MODEL // (2 * HEAD_DIM)
A_KV_HEADS = 2
A_GROUP = A_HEADS // A_KV_HEADS
B_HEADS = D_MODEL // (4 * HEAD_DIM)
AB_SPLITS = (A_HEADS * HEAD_DIM, A_KV_HEADS * HEAD_DIM, A_KV_HEADS * HEAD_DIM,
             B_HEADS * 2 * HEAD_DIM, B_HEADS * 2 * HEAD_DIM, B_HEADS * 2 * HEAD_DIM)
AB_IN = sum(AB_SPLITS)
AB_OUT = A_HEADS * HEAD_DIM + B_HEADS * 2 * HEAD_DIM
C_HEADS = D_MODEL // HEAD_DIM
NA_ROWS = 8
NA_COLS = 16
PEER_HEADS = 8
PEER_N_KEYS = 128
PEER_N_EXPERTS = PEER_N_KEYS * PEER_N_KEYS
PEER_TOPK = 16
PEER_QDIM = 256
PEER_CHUNK = 128

kernel_name = "hybrid_diffusion_gqa_diffattn_natten_peer"


def rms_norm(x, g):
    xf = x.astype(jnp.float32)
    y = xf * lax.rsqrt(jnp.mean(xf * xf, axis=-1, keepdims=True) + EPS)
    return (y * g.astype(jnp.float32)).astype(x.dtype)


def axial_rope_tables(n_tokens):
    t = jnp.arange(n_tokens)
    row = (t // GRID_W).astype(jnp.float32)
    col = (t % GRID_W).astype(jnp.float32)
    axis_dim = HEAD_DIM // 2
    freqs = 1.0 / (ROPE_THETA ** (jnp.arange(0, axis_dim, 2, dtype=jnp.float32) / axis_dim))
    ang = jnp.concatenate([row[:, None] * freqs, col[:, None] * freqs], axis=-1)
    return jnp.cos(ang), jnp.sin(ang)


def apply_axial_rope(x, cos, sin):
    B, S, H, D = x.shape
    xp = x.astype(jnp.float32).reshape(B, S, H, 2, 2, D // 4)
    c = cos.reshape(S, 1, 2, D // 4)
    s = sin.reshape(S, 1, 2, D // 4)
    x1 = xp[..., 0, :]
    x2 = xp[..., 1, :]
    out = jnp.stack([x1 * c - x2 * s, x1 * s + x2 * c], axis=-2)
    return out.reshape(B, S, H, D).astype(x.dtype)


def map_query_blocks(fn, qs):
    B, S = qs[0].shape[:2]
    nb = S // Q_BLOCK
    blocks = tuple(jnp.moveaxis(q.reshape((B, nb, Q_BLOCK) + q.shape[2:]), 1, 0) for q in qs)
    outs = lax.map(fn, blocks)
    return tuple(jnp.moveaxis(o, 0, 1).reshape((B, S) + o.shape[3:]) for o in outs)


def gqa_attend(q, k, v):
    s = jnp.einsum('bqhgd,bkhd->bhgqk', q, k) * (HEAD_DIM ** -0.5)
    p = jax.nn.softmax(s.astype(jnp.float32), axis=-1).astype(v.dtype)
    return jnp.einsum('bhgqk,bkhd->bqhgd', p, v)


def diff_attend(q, k, v, lam):
    s = jnp.einsum('bqhcd,bkhcd->bchqk', q, k) * (HEAD_DIM ** -0.5)
    p = jax.nn.softmax(s.astype(jnp.float32), axis=-1)
    p = (p[:, 0] - lam * p[:, 1]).astype(v.dtype)
    return jnp.einsum('bhqk,bkhe->bqhe', p, v)


def mixer_ab(h_lat, h_ctx, w_in, w_out, a_qn, a_kn, b_qn, b_kn, b_lam, b_subln, lam_init, cos, sin, need_ctx_out):
    def project(h, rope):
        B, L = h.shape[:2]
        qa, ka, va, qb, kb, vb = jnp.split(h @ w_in, np.cumsum(AB_SPLITS)[:-1].tolist(), axis=-1)
        qa = rms_norm(qa.reshape(B, L, A_HEADS, HEAD_DIM), a_qn)
        ka = rms_norm(ka.reshape(B, L, A_KV_HEADS, HEAD_DIM), a_kn)
        va = va.reshape(B, L, A_KV_HEADS, HEAD_DIM)
        qb = rms_norm(qb.reshape(B, L, 2 * B_HEADS, HEAD_DIM), b_qn)
        kb = rms_norm(kb.reshape(B, L, 2 * B_HEADS, HEAD_DIM), b_kn)
        vb = vb.reshape(B, L, B_HEADS, 2 * HEAD_DIM)
        if rope:
            qa, ka = apply_axial_rope(qa, cos, sin), apply_axial_rope(ka, cos, sin)
            qb, kb = apply_axial_rope(qb, cos, sin), apply_axial_rope(kb, cos, sin)
        qa = qa.reshape(B, L, A_KV_HEADS, A_GROUP, HEAD_DIM)
        qb = qb.reshape(B, L, B_HEADS, 2, HEAD_DIM)
        kb = kb.reshape(B, L, B_HEADS, 2, HEAD_DIM)
        return qa, ka, va, qb, kb, vb

    lf = b_lam.astype(jnp.float32)
    lam = jnp.exp(jnp.sum(lf[0] * lf[1])) - jnp.exp(jnp.sum(lf[2] * lf[3])) + lam_init

    qa_l, ka_l, va_l, qb_l, kb_l, vb_l = project(h_lat, True)
    qa_c, ka_c, va_c, qb_c, kb_c, vb_c = project(h_ctx, False)
    ka_all = jnp.concatenate([ka_c, ka_l], axis=1)
    va_all = jnp.concatenate([va_c, va_l], axis=1)
    kb_all = jnp.concatenate([kb_c, kb_l], axis=1)
    vb_all = jnp.concatenate([vb_c, vb_l], axis=1)

    def block(qs):
        qa_blk, qb_blk = qs
        return gqa_attend(qa_blk, ka_all, va_all), diff_attend(qb_blk, kb_all, vb_all, lam)

    def merge(oa, ob):
        B, L = oa.shape[:2]
        ob = rms_norm(ob, b_subln) * (1.0 - lam_init)
        o = jnp.concatenate([oa.reshape(B, L, -1), ob.reshape(B, L, -1)], axis=-1)
        return o @ w_out

    oa_l, ob_l = map_query_blocks(block, (qa_l, qb_l))
    o_lat = merge(oa_l, ob_l)
    o_ctx = None
    if need_ctx_out:
        o_ctx = merge(gqa_attend(qa_c, ka_c, va_c), diff_attend(qb_c, kb_c, vb_c, lam))
    return o_lat, o_ctx


def mixer_c(h_lat, h_ctx, w_in, w_out, qn, kn, rpb, need_ctx_out):
    def project(h):
        B, L = h.shape[:2]
        q, k, v = jnp.split(h @ w_in, 3, axis=-1)
        q = rms_norm(q.reshape(B, L, C_HEADS, HEAD_DIM), qn)
        k = rms_norm(k.reshape(B, L, C_HEADS, HEAD_DIM), kn)
        return q, k, v.reshape(B, L, C_HEADS, HEAD_DIM)

    q, k, v = project(h_lat)
    qc, kc, vc = project(h_ctx)
    B, S = h_lat.shape[:2]
    rows = S // GRID_W
    wr = min(NA_ROWS, rows)
    scale = HEAD_DIM ** -0.5
    qg = q.reshape(B, rows, GRID_W, C_HEADS, HEAD_DIM)
    kg = k.reshape(B, rows, GRID_W, C_HEADS, HEAD_DIM)
    vg = v.reshape(B, rows, GRID_W, C_HEADS, HEAD_DIM)
    r_all = jnp.arange(rows)
    row_start = jnp.clip(r_all - wr // 2, 0, rows - wr)
    j = jnp.arange(GRID_W)
    col_start = jnp.clip(j - NA_COLS // 2, 0, GRID_W - NA_COLS)
    col_idx = col_start[:, None] + jnp.arange(NA_COLS)
    dc_idx = col_idx - j[:, None] + (NA_COLS - 1)

    def row_block(args):
        q_row, r_i, rs_i = args
        band_k = lax.dynamic_slice_in_dim(kg, rs_i, wr, axis=1)
        band_v = lax.dynamic_slice_in_dim(vg, rs_i, wr, axis=1)
        kw = band_k[:, :, col_idx]
        vw = band_v[:, :, col_idx]
        dr_idx = rs_i + jnp.arange(wr) - r_i + (NA_ROWS - 1)
        bias = jnp.transpose(rpb[:, dr_idx][:, :, dc_idx], (0, 2, 1, 3))
        s_win = jnp.einsum('bqhd,brqchd->bhqrc', q_row, kw) * scale + bias[None]
        s_win = s_win.reshape(B, C_HEADS, GRID_W, wr * NA_COLS)
        s_ctx = jnp.einsum('bqhd,bkhd->bhqk', q_row, kc) * scale
        p = jax.nn.softmax(jnp.concatenate([s_win, s_ctx], axis=-1).astype(jnp.float32), axis=-1)
        p = p.astype(v.dtype)
        p_win = p[..., :wr * NA_COLS].reshape(B, C_HEADS, GRID_W, wr, NA_COLS)
        p_ctx = p[..., wr * NA_COLS:]
        return (jnp.einsum('bhqrc,brqchd->bqhd', p_win, vw)
                + jnp.einsum('bhqk,bkhd->bqhd', p_ctx, vc))

    o = lax.map(row_block, (jnp.moveaxis(qg, 1, 0), r_all, row_start))
    o_lat = jnp.moveaxis(o, 0, 1).reshape(B, S, C_HEADS * HEAD_DIM) @ w_out
    o_ctx = None
    if need_ctx_out:
        oc = gqa_attend(qc[:, :, :, None], kc, vc)
        o_ctx = oc.reshape(B, qc.shape[1], C_HEADS * HEAD_DIM) @ w_out
    return o_lat, o_ctx


def peer_ffn(h, w_q, sub_keys, expert_u, expert_v):
    B, L, D = h.shape
    xs = h.reshape(-1, PEER_CHUNK, D)

    def chunk(xc):
        T = xc.shape[0]
        qry = (xc @ w_q).reshape(T, PEER_HEADS, 2, PEER_QDIM // 2)
        s = jnp.einsum('thpd,hpnd->thpn', qry, sub_keys).astype(jnp.float32)
        s1, i1 = lax.top_k(s[:, :, 0], PEER_TOPK)
        s2, i2 = lax.top_k(s[:, :, 1], PEER_TOPK)
        cand = (s1[..., :, None] + s2[..., None, :]).reshape(T, PEER_HEADS, PEER_TOPK * PEER_TOPK)
        cand_idx = (i1[..., :, None] * PEER_N_KEYS + i2[..., None, :]).reshape(T, PEER_HEADS, PEER_TOPK * PEER_TOPK)
        top_s, top_pos = lax.top_k(cand, PEER_TOPK)
        eidx = jnp.take_along_axis(cand_idx, top_pos, axis=-1)
        g = jax.nn.softmax(top_s, axis=-1)
        u = jnp.take(expert_u, eidx, axis=0)
        a = jax.nn.gelu(jnp.einsum('td,thkd->thk', xc, u).astype(jnp.float32))
        v = jnp.take(expert_v, eidx, axis=0)
        return jnp.einsum('thk,thkd->td', (g * a).astype(xc.dtype), v)

    return lax.map(chunk, xs).reshape(B, L, D)


def setup_inputs(seed: int = 0) -> dict:
    key = jax.random.key(seed)
    ks = jax.random.split(key, 24)
    n_even = (DEPTH + 1) // 2
    n_odd = DEPTH // 2
    f32 = jnp.float32
    nrm = lambda k, shape, s: jax.random.normal(k, shape, f32) * s
    gain = lambda k, shape: 1.0 + 0.05 * jax.random.normal(k, shape, f32)
    return {
        "x": nrm(ks[0], (BATCH, SEQ, D_MODEL), 1.0),
        "c": nrm(ks[1], (BATCH, D_MODEL), 1.0),
        "ctx": nrm(ks[2], (BATCH, CTX_LEN, D_MODEL), 1.0),
        "c_ctx": nrm(ks[3], (D_MODEL,), 1.0),
        "ada_w": nrm(ks[4], (DEPTH, D_MODEL, 6 * D_MODEL), D_MODEL ** -0.5),
        "ada_b": nrm(ks[5], (DEPTH, 6 * D_MODEL), 0.02),
        "norm_g": gain(ks[6], (DEPTH, 2, D_MODEL)),
        "ab_w_in": nrm(ks[7], (n_even, D_MODEL, AB_IN), D_MODEL ** -0.5),
        "ab_w_out": nrm(ks[8], (n_even, AB_OUT, D_MODEL), AB_OUT ** -0.5),
        "a_q_norm": gain(ks[9], (n_even, HEAD_DIM)),
        "a_k_norm": gain(ks[10], (n_even, HEAD_DIM)),
        "b_q_norm": gain(ks[11], (n_even, HEAD_DIM)),
        "b_k_norm": gain(ks[12], (n_even, HEAD_DIM)),
        "b_lambda": nrm(ks[13], (n_even, 4, HEAD_DIM), 0.1),
        "b_subln": gain(ks[14], (n_even, 2 * HEAD_DIM)),
        "c_w_in": nrm(ks[15], (n_odd, D_MODEL, 3 * C_HEADS * HEAD_DIM), D_MODEL ** -0.5),
        "c_w_out": nrm(ks[16], (n_odd, C_HEADS * HEAD_DIM, D_MODEL), (C_HEADS * HEAD_DIM) ** -0.5),
        "c_q_norm": gain(ks[17], (n_odd, HEAD_DIM)),
        "c_k_norm": gain(ks[18], (n_odd, HEAD_DIM)),
        "c_rpb": nrm(ks[19], (n_odd, C_HEADS, 2 * NA_ROWS - 1, 2 * NA_COLS - 1), 0.2),
        "peer_w_q": nrm(ks[20], (DEPTH, D_MODEL, PEER_HEADS * PEER_QDIM), D_MODEL ** -0.5),
        "peer_keys": nrm(ks[21], (DEPTH, PEER_HEADS, 2, PEER_N_KEYS, PEER_QDIM // 2), (PEER_QDIM // 2) ** -0.5),
        "peer_u": nrm(ks[22], (DEPTH, PEER_N_EXPERTS, D_MODEL), D_MODEL ** -0.5),
        "peer_v": nrm(ks[23], (DEPTH, PEER_N_EXPERTS, D_MODEL), 0.1),
    }


def reference(x, c, ctx, c_ctx, ada_w, ada_b, norm_g, ab_w_in, ab_w_out, a_q_norm, a_k_norm,
              b_q_norm, b_k_norm, b_lambda, b_subln, c_w_in, c_w_out, c_q_norm, c_k_norm, c_rpb,
              peer_w_q, peer_keys, peer_u, peer_v):
    B, S, D = x.shape
    cos, sin = axial_rope_tables(S)
    for l in range(DEPTH):
        last = l == DEPTH - 1
        m_lat = (jax.nn.silu(c) @ ada_w[l] + ada_b[l]).reshape(B, 6, 1, D)
        m_ctx = (jax.nn.silu(c_ctx) @ ada_w[l] + ada_b[l]).reshape(6, 1, D)
        h_lat = rms_norm(x, norm_g[l, 0]) * (1.0 + m_lat[:, 1]) + m_lat[:, 0]
        h_ctx = rms_norm(ctx, norm_g[l, 0]) * (1.0 + m_ctx[1]) + m_ctx[0]
        i = l // 2
        if l % 2 == 0:
            lam_init = 0.8 - 0.6 * math.exp(-0.3 * l)
            o_lat, o_ctx = mixer_ab(h_lat, h_ctx, ab_w_in[i], ab_w_out[i], a_q_norm[i], a_k_norm[i],
                                    b_q_norm[i], b_k_norm[i], b_lambda[i], b_subln[i], lam_init,
                                    cos, sin, not last)
        else:
            o_lat, o_ctx = mixer_c(h_lat, h_ctx, c_w_in[i], c_w_out[i], c_q_norm[i], c_k_norm[i],
                                   c_rpb[i], not last)
        x = x + m_lat[:, 2] * o_lat
        h_lat = rms_norm(x, norm_g[l, 1]) * (1.0 + m_lat[:, 4]) + m_lat[:, 3]
        x = x + m_lat[:, 5] * peer_ffn(h_lat, peer_w_q[l], peer_keys[l], peer_u[l], peer_v[l])
        if not last:
            ctx = ctx + m_ctx[2] * o_ctx
            h_ctx = rms_norm(ctx, norm_g[l, 1]) * (1.0 + m_ctx[4]) + m_ctx[3]
            ctx = ctx + m_ctx[5] * peer_ffn(h_ctx, peer_w_q[l], peer_keys[l], peer_u[l], peer_v[l])
    return x
```

```python
import functools
import math

import numpy as np
import jax
import jax.numpy as jnp
from jax import lax
from jax.experimental import pallas as pl
from jax.experimental.pallas import tpu as pltpu

F32 = jnp.float32
BF16 = jnp.bfloat16

D_MODEL = 1024
GRID_W = 64
HEAD_DIM = 64
ROPE_THETA = 10000.0
EPS = 1e-6
A_HEADS = 8
A_KV_HEADS = 2
B_HEADS = 4
C_HEADS = 16
NA_ROWS = 8
NA_COLS = 16
PEER_HEADS = 8
PEER_N_KEYS = 128
PEER_TOPK = 16

LANES = 128
SUBLANES = 8
VMEM_LIMIT = 56 * 1024 * 1024
NEG = -1e30

ROW_TILE = 256
Q_TILE = 256
K_CHUNK = 512
NA_BLOCK_ROWS = 4
PEER_TOKENS = 512
PEER_EXPERTS = 512


def _params(*sem):
    return pltpu.CompilerParams(dimension_semantics=sem, vmem_limit_bytes=VMEM_LIMIT)


def _ada_kernel(cs_ref, w_ref, b_ref, o_ref):
    cs = cs_ref[...]
    a = cs * jax.nn.sigmoid(cs)
    o_ref[0] = jnp.dot(a, w_ref[0], preferred_element_type=F32,
                       precision=lax.Precision.HIGHEST) + b_ref[0]


def _ada(cs, ada_w, ada_b):
    depth, d, n = ada_w.shape
    tn = 1536
    return pl.pallas_call(
        _ada_kernel,
        grid=(depth, n // tn),
        in_specs=[pl.BlockSpec((SUBLANES, d), lambda l, j: (0, 0)),
                  pl.BlockSpec((1, d, tn), lambda l, j: (l, 0, j)),
                  pl.BlockSpec((1, 1, tn), lambda l, j: (l, 0, j))],
        out_specs=pl.BlockSpec((1, SUBLANES, tn), lambda l, j: (l, 0, j)),
        out_shape=jax.ShapeDtypeStruct((depth, SUBLANES, n), F32),
        compiler_params=_params("parallel", "parallel"),
    )(cs, ada_w, ada_b.reshape(depth, 1, n))


def _modulated_norm(x, gain, shift, scale):
    ms = jnp.mean(x * x, axis=-1, keepdims=True)
    return (x * lax.rsqrt(ms + EPS)) * gain * (1.0 + scale) + shift


def _proj_kernel(*refs, nq, nk, rope):
    if rope:
        (x_ref, mod_ref, g_ref, w_ref, gq_ref, gk_ref, bd_ref, cos_ref, sa_ref, sb_ref,
         q_ref, kt_ref, v_ref) = refs
    else:
        x_ref, mod_ref, g_ref, w_ref, gq_ref, gk_ref, bd_ref, q_ref, kt_ref, v_ref = refs
    h = _modulated_norm(x_ref[0], g_ref[...], mod_ref[0, 0:1, :], mod_ref[0, 1:2, :])
    p = jnp.dot(h.astype(BF16), w_ref[...], preferred_element_type=F32)
    bd = bd_ref[...]

    def head_norm(c0, gain):
        yc = p[:, c0:c0 + LANES]
        ss = jnp.dot((yc * yc).astype(BF16), bd, preferred_element_type=F32)
        yn = yc * lax.rsqrt(ss + EPS) * gain
        if rope:
            yn = (yn * cos_ref[...] + pltpu.roll(yn, LANES - 16, 1) * sa_ref[...]
                  + pltpu.roll(yn, 16, 1) * sb_ref[...])
        return yn

    for c in range(nq // LANES):
        c0 = c * LANES
        q_ref[0, :, c0:c0 + LANES] = head_norm(c0, gq_ref[:, c0:c0 + LANES]).astype(BF16)
    for c in range(nk // LANES):
        c0 = c * LANES
        yn = head_norm(nq + c0, gk_ref[:, c0:c0 + LANES])
        kt_ref[0, c0:c0 + LANES, :] = yn.T.astype(BF16)
    v_ref[0] = p[:, nq + nk:].astype(BF16)


def _proj(x, mod, gain, w, gq, gk, bd, rope_tabs, nq, nk, nv):
    b, l, d = x.shape
    tm = ROW_TILE
    n = nq + nk + nv
    mod_map = (lambda i, j: (i, 0, 0)) if mod.shape[0] > 1 else (lambda i, j: (0, 0, 0))
    in_specs = [pl.BlockSpec((1, tm, d), lambda i, j: (i, j, 0)),
                pl.BlockSpec((1, 6, d), mod_map),
                pl.BlockSpec((1, d), lambda i, j: (0, 0)),
                pl.BlockSpec((d, n), lambda i, j: (0, 0)),
                pl.BlockSpec((1, nq), lambda i, j: (0, 0)),
                pl.BlockSpec((1, nk), lambda i, j: (0, 0)),
                pl.BlockSpec((LANES, LANES), lambda i, j: (0, 0))]
    args = [x, mod, gain, w, gq, gk, bd]
    if rope_tabs is not None:
        in_specs += [pl.BlockSpec((tm, LANES), lambda i, j: (j, 0))] * 3
        args += list(rope_tabs)
    return pl.pallas_call(
        functools.partial(_proj_kernel, nq=nq, nk=nk, rope=rope_tabs is not None),
        grid=(b, l // tm),
        in_specs=in_specs,
        out_specs=[pl.BlockSpec((1, tm, nq), lambda i, j: (i, j, 0)),
                   pl.BlockSpec((1, nk, tm), lambda i, j: (i, 0, j)),
                   pl.BlockSpec((1, tm, nv), lambda i, j: (i, j, 0))],
        out_shape=[jax.ShapeDtypeStruct((b, l, nq), BF16),
                   jax.ShapeDtypeStruct((b, nk, l), BF16),
                   jax.ShapeDtypeStruct((b, l, nv), BF16)],
        compiler_params=_params("parallel", "parallel"),
    )(*args)


def _flash_init(m_ref, l_ref, acc_ref):
    m_ref[...] = jnp.full(m_ref.shape, -jnp.inf, F32)
    l_ref[...] = jnp.zeros(l_ref.shape, F32)
    acc_ref[...] = jnp.zeros(acc_ref.shape, F32)


def _flash_step(i, q, kt, v, m_ref, l_ref, acc_ref):
    s = jnp.dot(q, kt, preferred_element_type=F32)
    m_prev = m_ref[i]
    m_new = jnp.maximum(m_prev, jnp.max(s, axis=-1, keepdims=True))
    alpha = jnp.exp(m_prev - m_new)
    p = jnp.exp(s - m_new)
    l_ref[i] = alpha * l_ref[i] + jnp.sum(p, axis=-1, keepdims=True)
    acc_ref[i] = alpha * acc_ref[i] + jnp.dot(p.astype(BF16), v, preferred_element_type=F32)
    m_ref[i] = m_new


def _flash_sweep(q_ref, ktl_ref, vl_ref, ktc_ref, vc_ref, m_ref, l_ref, acc_ref, n_maps, kt_row):
    _flash_init(m_ref, l_ref, acc_ref)

    def maps(kt, v):
        for i in range(n_maps):
            q = q_ref[0, :, i * HEAD_DIM:(i + 1) * HEAD_DIM]
            r0 = kt_row(i)
            _flash_step(i, q, kt[r0:r0 + HEAD_DIM, :], v, m_ref, l_ref, acc_ref)

    if ktl_ref is not None:
        s_len = ktl_ref.shape[2]
        tk = min(K_CHUNK, s_len)

        def body(c, carry):
            off = pl.multiple_of(c * tk, tk)
            maps(ktl_ref[0, :, pl.ds(off, tk)], vl_ref[0, pl.ds(off, tk), :])
            return carry

        lax.fori_loop(0, s_len // tk, body, 0)
    maps(ktc_ref[0], vc_ref[0])


def _gqa_kernel(*refs, has_lat):
    if has_lat:
        q_ref, ktl_ref, vl_ref, ktc_ref, vc_ref, o_ref, m_ref, l_ref, acc_ref = refs
    else:
        q_ref, ktc_ref, vc_ref, o_ref, m_ref, l_ref, acc_ref = refs
        ktl_ref = vl_ref = None
    group = A_HEADS // A_KV_HEADS
    _flash_sweep(q_ref, ktl_ref, vl_ref, ktc_ref, vc_ref, m_ref, l_ref, acc_ref, A_HEADS,
                 lambda i: (i // group) * HEAD_DIM)
    lane = lax.broadcasted_iota(jnp.int32, (q_ref.shape[1], LANES), 1)
    for j in range(A_HEADS // 2):
        g = (2 * j) // group
        a = acc_ref[2 * j] / l_ref[2 * j]
        c = acc_ref[2 * j + 1] / l_ref[2 * j + 1]
        if g == 0:
            c = pltpu.roll(c, HEAD_DIM, 1)
        else:
            a = pltpu.roll(a, HEAD_DIM, 1)
        o_ref[0, :, j * LANES:(j + 1) * LANES] = jnp.where(lane < HEAD_DIM, a, c).astype(BF16)


def _diff_kernel(*refs, has_lat, lam_init):
    if has_lat:
        q_ref, ktl_ref, vl_ref, ktc_ref, vc_ref, lam_ref, sub_ref, o_ref, m_ref, l_ref, acc_ref = refs
    else:
        q_ref, ktc_ref, vc_ref, lam_ref, sub_ref, o_ref, m_ref, l_ref, acc_ref = refs
        ktl_ref = vl_ref = None
    _flash_sweep(q_ref, ktl_ref, vl_ref, ktc_ref, vc_ref, m_ref, l_ref, acc_ref, 2,
                 lambda i: i * HEAD_DIM)
    lf = lam_ref[...]
    lam = (jnp.exp(jnp.sum(lf[0:1] * lf[1:2], axis=-1, keepdims=True))
           - jnp.exp(jnp.sum(lf[2:3] * lf[3:4], axis=-1, keepdims=True)) + lam_init)
    o = acc_ref[0] / l_ref[0] - lam * (acc_ref[1] / l_ref[1])
    ms = jnp.mean(o * o, axis=-1, keepdims=True)
    o = (o * lax.rsqrt(ms + EPS)) * sub_ref[...] * (1.0 - lam_init)
    o_ref[0] = o.astype(BF16)


def _flash_scratch(n_maps, tq):
    return [pltpu.VMEM((n_maps, tq, 1), F32), pltpu.VMEM((n_maps, tq, 1), F32),
            pltpu.VMEM((n_maps, tq, LANES), F32)]


def _gqa(q, kt_lat, v_lat, kt_ctx, v_ctx):
    b, lq, _ = q.shape
    tq = min(Q_TILE, lq)
    nc = kt_ctx.shape[2]
    has_lat = kt_lat is not None
    in_specs = [pl.BlockSpec((1, tq, A_HEADS * HEAD_DIM), lambda i, j: (i, j, 0))]
    args = [q]
    if has_lat:
        s = kt_lat.shape[2]
        in_specs += [pl.BlockSpec((1, LANES, s), lambda i, j: (i, 0, 0)),
                     pl.BlockSpec((1, s, LANES), lambda i, j: (i, 0, 0))]
        args += [kt_lat, v_lat]
    in_specs += [pl.BlockSpec((1, LANES, nc), lambda i, j: (i, 0, 0)),
                 pl.BlockSpec((1, nc, LANES), lambda i, j: (i, 0, 0))]
    args += [kt_ctx, v_ctx]
    return pl.pallas_call(
        functools.partial(_gqa_kernel, has_lat=has_lat),
        grid=(b, lq // tq),
        in_specs=in_specs,
        out_specs=pl.BlockSpec((1, tq, A_HEADS * HEAD_DIM), lambda i, j: (i, j, 0)),
        out_shape=jax.ShapeDtypeStruct((b, lq, A_HEADS * HEAD_DIM), BF16),
        scratch_shapes=_flash_scratch(A_HEADS, tq),
        compiler_params=_params("parallel", "parallel"),
    )(*args)


def _diff(q, kt_lat, v_lat, kt_ctx, v_ctx, b_lambda, b_subln, lam_init):
    b, lq, _ = q.shape
    tq = min(Q_TILE, lq)
    nc = kt_ctx.shape[2]
    has_lat = kt_lat is not None
    q0 = A_HEADS * HEAD_DIM // LANES
    in_specs = [pl.BlockSpec((1, tq, LANES), lambda i, h, j: (i, j, q0 + h))]
    args = [q]
    if has_lat:
        s = kt_lat.shape[2]
        in_specs += [pl.BlockSpec((1, LANES, s), lambda i, h, j: (i, 1 + h, 0)),
                     pl.BlockSpec((1, s, LANES), lambda i, h, j: (i, 0, 1 + h))]
        args += [kt_lat, v_lat]
    in_specs += [pl.BlockSpec((1, LANES, nc), lambda i, h, j: (i, 1 + h, 0)),
                 pl.BlockSpec((1, nc, LANES), lambda i, h, j: (i, 0, 1 + h)),
                 pl.BlockSpec((4, HEAD_DIM), lambda i, h, j: (0, 0)),
                 pl.BlockSpec((1, LANES), lambda i, h, j: (0, 0))]
    args += [kt_ctx, v_ctx, b_lambda, b_subln]
    return pl.pallas_call(
        functools.partial(_diff_kernel, has_lat=has_lat, lam_init=lam_init),
        grid=(b, B_HEADS, lq // tq),
        in_specs=in_specs,
        out_specs=pl.BlockSpec((1, tq, LANES), lambda i, h, j: (i, j, h)),
        out_shape=jax.ShapeDtypeStruct((b, lq, B_HEADS * LANES), BF16),
        scratch_shapes=_flash_scratch(2, tq),
        compiler_params=_params("parallel", "parallel", "parallel"),
    )(*args)


def _na_kernel(q_ref, km_ref, k0_ref, kp_ref, kc_ref, vm_ref, v0_ref, vp_ref, vc_ref, bias_ref,
               o_ref, *, rows):
    i = pl.program_id(1)
    tq = q_ref.shape[1]
    nb = 3 * tq
    qrow = NA_BLOCK_ROWS * i + lax.broadcasted_iota(jnp.int32, (tq, nb), 0) // GRID_W
    krow = NA_BLOCK_ROWS * (i - 1) + lax.broadcasted_iota(jnp.int32, (tq, nb), 1) // GRID_W
    rs = jnp.clip(qrow - NA_ROWS // 2, 0, rows - NA_ROWS)
    row_mask = jnp.where((krow >= rs) & (krow < rs + NA_ROWS), 0.0, NEG).astype(F32)
    lane = lax.broadcasted_iota(jnp.int32, (tq, LANES), 1)
    kts = (km_ref, k0_ref, kp_ref)
    vs = (vm_ref, v0_ref, vp_ref)
    for pair in range(C_HEADS // 2):
        outs = []
        for h in (2 * pair, 2 * pair + 1):
            r0 = h * HEAD_DIM
            q = q_ref[0, :, r0:r0 + HEAD_DIM]
            s_band = jnp.concatenate(
                [jnp.dot(q, k[0, r0:r0 + HEAD_DIM, :], preferred_element_type=F32) for k in kts],
                axis=1) + bias_ref[h] + row_mask
            s_ctx = jnp.dot(q, kc_ref[0, r0:r0 + HEAD_DIM, :], preferred_element_type=F32)
            m = jnp.maximum(jnp.max(s_band, axis=-1, keepdims=True),
                            jnp.max(s_ctx, axis=-1, keepdims=True))
            p_band = jnp.exp(s_band - m)
            p_ctx = jnp.exp(s_ctx - m)
            denom = jnp.sum(p_band, axis=-1, keepdims=True) + jnp.sum(p_ctx, axis=-1, keepdims=True)
            c0 = pair * LANES
            pv = jnp.dot(p_ctx.astype(BF16), vc_ref[0, :, c0:c0 + LANES], preferred_element_type=F32)
            for t, v in enumerate(vs):
                pv += jnp.dot(p_band[:, t * tq:(t + 1) * tq].astype(BF16), v[0, :, c0:c0 + LANES],
                              preferred_element_type=F32)
            outs.append(pv / denom)
        a, c = outs
        o_ref[0, :, pair * LANES:(pair + 1) * LANES] = jnp.where(lane < HEAD_DIM, a, c).astype(BF16)


def _na(q, kt_lat, v_lat, kt_ctx, v_ctx, bias):
    b, s, w = q.shape
    tq = NA_BLOCK_ROWS * GRID_W
    nblk = s // tq
    nc = kt_ctx.shape[2]
    rows = s // GRID_W

    def kt_spec(d):
        return pl.BlockSpec((1, w, tq), lambda i, j: (i, 0, jnp.clip(j + d, 0, nblk - 1)))

    def v_spec(d):
        return pl.BlockSpec((1, tq, w), lambda i, j: (i, jnp.clip(j + d, 0, nblk - 1), 0))

    return pl.pallas_call(
        functools.partial(_na_kernel, rows=rows),
        grid=(b, nblk),
        in_specs=[pl.BlockSpec((1, tq, w), lambda i, j: (i, j, 0)),
                  kt_spec(-1), kt_spec(0), kt_spec(1),
                  pl.BlockSpec((1, w, nc), lambda i, j: (i, 0, 0)),
                  v_spec(-1), v_spec(0), v_spec(1),
                  pl.BlockSpec((1, nc, w), lambda i, j: (i, 0, 0)),
                  pl.BlockSpec(bias.shape, lambda i, j: (0, 0, 0))],
        out_specs=pl.BlockSpec((1, tq, w), lambda i, j: (i, j, 0)),
        out_shape=jax.ShapeDtypeStruct((b, s, w), BF16),
        compiler_params=_params("parallel", "parallel"),
    )(q, kt_lat, kt_lat, kt_lat, kt_ctx, v_lat, v_lat, v_lat, v_ctx, bias)


def _na_bias_table(rpb):
    tq = NA_BLOCK_ROWS * GRID_W
    qi = np.arange(tq)
    ki = np.arange(3 * tq)
    qr, qc = qi // GRID_W, qi % GRID_W
    kr, kc = ki // GRID_W - NA_BLOCK_ROWS, ki % GRID_W
    dr = kr[None, :] - qr[:, None] + (NA_ROWS - 1)
    dc = kc[None, :] - qc[:, None] + (NA_COLS - 1)
    cs = np.clip(qc - NA_COLS // 2, 0, GRID_W - NA_COLS)
    col_ok = (kc[None, :] >= cs[:, None]) & (kc[None, :] < cs[:, None] + NA_COLS)
    dr = np.clip(dr, 0, 2 * NA_ROWS - 2)
    dc = np.clip(dc, 0, 2 * NA_COLS - 2)
    return jnp.where(col_ok[None], rpb[:, dr, dc], NEG).astype(F32)


def _out_kernel(o1_ref, o2_ref, w_ref, x_ref, mod_ref, g_ref, x1_ref, ht_ref):
    half = o1_ref.shape[2]
    o = (jnp.dot(o1_ref[0], w_ref[0:half, :], preferred_element_type=F32)
         + jnp.dot(o2_ref[0], w_ref[half:, :], preferred_element_type=F32))
    x1 = x_ref[0] + mod_ref[0, 2:3, :] * o
    x1_ref[0] = x1
    h2 = _modulated_norm(x1, g_ref[...], mod_ref[0, 3:4, :], mod_ref[0, 4:5, :])
    ht_ref[0] = h2.T.astype(BF16)


def _out(o1, o1_blk, o2, o2_blk, w, x, mod, gain):
    b, l, d = x.shape
    tm = min(ROW_TILE, l)
    half = d // 2
    mod_map = (lambda i, j: (i, 0, 0)) if mod.shape[0] > 1 else (lambda i, j: (0, 0, 0))
    return pl.pallas_call(
        _out_kernel,
        grid=(b, l // tm),
        in_specs=[pl.BlockSpec((1, tm, half), lambda i, j: (i, j, o1_blk)),
                  pl.BlockSpec((1, tm, half), lambda i, j: (i, j, o2_blk)),
                  pl.BlockSpec((d, d), lambda i, j: (0, 0)),
                  pl.BlockSpec((1, tm, d), lambda i, j: (i, j, 0)),
                  pl.BlockSpec((1, 6, d), mod_map),
                  pl.BlockSpec((1, d), lambda i, j: (0, 0))],
        out_specs=[pl.BlockSpec((1, tm, d), lambda i, j: (i, j, 0)),
                   pl.BlockSpec((1, d, tm), lambda i, j: (i, 0, j))],
        out_shape=[jax.ShapeDtypeStruct((b, l, d), F32),
                   jax.ShapeDtypeStruct((b, d, l), BF16)],
        compiler_params=_params("parallel", "parallel"),
    )(o1, o2, w, x, mod, gain)


def _top16(s, iota):
    rank = jnp.full(s.shape, float(PEER_TOPK), F32)
    big = float(s.shape[0])
    vals = []
    for r in range(PEER_TOPK):
        m = jnp.max(s, axis=0, keepdims=True)
        idx = jnp.min(jnp.where(s == m, iota, big), axis=0, keepdims=True)
        hit = iota == idx
        rank = jnp.where(hit, float(r), rank)
        s = jnp.where(hit, -jnp.inf, s)
        vals.append(m)
    return vals, rank


def _rows_tile(rows, iota8):
    t = jnp.zeros((SUBLANES,) + rows[0].shape[1:], F32)
    for k, r in enumerate(rows):
        t = jnp.where(iota8 == k, r, t)
    return t


_CAND_COUNTS = [PEER_TOPK // (a + 1) for a in range(SUBLANES)]


def _topk_kernel(ht_ref, wq_ref, keys_ref, n1_ref, c1_ref, b2_ref, e2_ref):
    t = ht_ref.shape[2]
    qt = jnp.dot(wq_ref[...], ht_ref[0], preferred_element_type=F32)
    iota = lax.broadcasted_iota(jnp.int32, (PEER_N_KEYS, t), 0).astype(F32)
    iota8 = lax.broadcasted_iota(jnp.int32, (SUBLANES, t), 0)
    half = PEER_N_KEYS
    for h in range(PEER_HEADS):
        q1 = qt[h * 2 * half:h * 2 * half + half].astype(BF16)
        q2 = qt[h * 2 * half + half:(h + 1) * 2 * half].astype(BF16)
        s1 = jnp.dot(keys_ref[2 * h], q1, preferred_element_type=F32)
        s2 = jnp.dot(keys_ref[2 * h + 1], q2, preferred_element_type=F32)
        v1, r1 = _top16(s1, iota)
        v2, r2 = _top16(s2, iota)
        e1 = [jnp.exp(v - v1[0]) for v in v1]
        e2 = [jnp.exp(v - v2[0]) for v in v2]
        v2_lo, v2_hi = _rows_tile(v2[:8], iota8), _rows_tile(v2[8:], iota8)
        e2_lo, e2_hi = _rows_tile(e2[:8], iota8), _rows_tile(e2[8:], iota8)
        v1_hi, e1_hi = _rows_tile(v1[8:], iota8), _rows_tile(e1[8:], iota8)
        cand, ecand = [], []
        for a, cnt in enumerate(_CAND_COUNTS):
            ok = iota8 < cnt
            cand.append(jnp.where(ok, v1[a] + v2_lo, -jnp.inf))
            ecand.append(e1[a] * e2_lo)
            if a == 0:
                cand.append(v1[0] + v2_hi)
                ecand.append(e1[0] * e2_hi)
        cand.append(v1_hi + v2[0])
        ecand.append(e1_hi * e2[0])
        cand = jnp.concatenate(cand, axis=0)
        ecand = jnp.concatenate(ecand, axis=0)
        iota_c = lax.broadcasted_iota(jnp.int32, cand.shape, 0).astype(F32)
        _, rc = _top16(cand, iota_c)
        sel = (rc < float(PEER_TOPK)).astype(F32)
        z = jnp.sum(sel * ecand, axis=0, keepdims=True)
        n_rows = [jnp.sum(sel[0:16], axis=0, keepdims=True)]
        for a in range(1, SUBLANES):
            n_rows.append(jnp.sum(sel[8 * (a + 1):8 * (a + 2)], axis=0, keepdims=True))
        n_hi = sel[72:80]
        n1 = jnp.zeros(s1.shape, F32)
        for a in range(PEER_TOPK):
            na = n_rows[a] if a < SUBLANES else n_hi[a - SUBLANES:a - SUBLANES + 1]
            n1 = jnp.where(r1 == float(a), na, n1)
        n1_ref[0, h] = n1
        c1_ref[0, h] = jnp.exp(s1 - v1[0]) / z
        b2_ref[0, h] = r2
        e2_ref[0, h] = jnp.exp(s2 - v2[0])


def _peer_topk(ht, wq_t, keys, tt):
    b, d, l = ht.shape
    nqd = wq_t.shape[0]
    shp = jax.ShapeDtypeStruct((b, PEER_HEADS, PEER_N_KEYS, l), F32)
    ospec = pl.BlockSpec((1, PEER_HEADS, PEER_N_KEYS, tt), lambda i, j: (i, 0, 0, j))
    return pl.pallas_call(
        _topk_kernel,
        grid=(b, l // tt),
        in_specs=[pl.BlockSpec((1, d, tt), lambda i, j: (i, 0, j)),
                  pl.BlockSpec((nqd, d), lambda i, j: (0, 0)),
                  pl.BlockSpec(keys.shape, lambda i, j: (0, 0, 0))],
        out_specs=[ospec] * 4,
        out_shape=[shp] * 4,
        compiler_params=_params("parallel", "parallel"),
    )(ht, wq_t, keys)


def _peer_kernel(ht_ref, n1_ref, c1_ref, b2_ref, e2_ref, u_ref, vt_ref, x_ref, mod_ref, o_ref,
                 a_ref, g_ref, acc_ref):
    e = pl.program_id(2)
    et = u_ref.shape[0]
    blocks = et // PEER_N_KEYS

    @pl.when(e == 0)
    def _():
        acc_ref[...] = jnp.zeros(acc_ref.shape, F32)

    a_ref[...] = jnp.dot(u_ref[...], ht_ref[0], preferred_element_type=F32)
    for k in range(blocks):
        i1 = e * blocks + k
        w = jnp.zeros((PEER_N_KEYS, ht_ref.shape[2]), F32)
        for h in range(PEER_HEADS):
            n_row = n1_ref[0, h, pl.ds(i1, 1), :]
            c_row = c1_ref[0, h, pl.ds(i1, 1), :]
            w = w + jnp.where(b2_ref[0, h] < n_row, e2_ref[0, h], 0.0) * c_row
        act = jax.nn.gelu(a_ref[k * PEER_N_KEYS:(k + 1) * PEER_N_KEYS, :])
        g_ref[k * PEER_N_KEYS:(k + 1) * PEER_N_KEYS, :] = (w * act).astype(BF16)
    acc_ref[...] += jnp.dot(vt_ref[...], g_ref[...], preferred_element_type=F32)

    @pl.when(e == pl.num_programs(2) - 1)
    def _():
        o_ref[0] = x_ref[0] + mod_ref[0, 5:6, :] * acc_ref[...].T


def _peer_dense(ht, n1, c1, b2, e2, u, vt, x, mod, tt):
    b, d, l = ht.shape
    ne = u.shape[0]
    et = PEER_EXPERTS
    mod_map = (lambda i, j, k: (i, 0, 0)) if mod.shape[0] > 1 else (lambda i, j, k: (0, 0, 0))
    sel_spec = pl.BlockSpec((1, PEER_HEADS, PEER_N_KEYS, tt), lambda i, j, k: (i, 0, 0, j))
    return pl.pallas_call(
        _peer_kernel,
        grid=(b, l // tt, ne // et),
        in_specs=[pl.BlockSpec((1, d, tt), lambda i, j, k: (i, 0, j)),
                  sel_spec, sel_spec, sel_spec, sel_spec,
                  pl.BlockSpec((et, d), lambda i, j, k: (k, 0)),
                  pl.BlockSpec((d, et), lambda i, j, k: (0, k)),
                  pl.BlockSpec((1, tt, d), lambda i, j, k: (i, j, 0)),
                  pl.BlockSpec((1, 6, d), mod_map)],
        out_specs=pl.BlockSpec((1, tt, d), lambda i, j, k: (i, j, 0)),
        out_shape=jax.ShapeDtypeStruct((b, l, d), F32),
        scratch_shapes=[pltpu.VMEM((et, tt), F32), pltpu.VMEM((et, tt), BF16),
                        pltpu.VMEM((d, tt), F32)],
        compiler_params=_params("parallel", "parallel", "arbitrary"),
    )(ht, n1, c1, b2, e2, u, vt, x, mod)


def _peer(ht, x1, mod, wq_t, keys, u, vt):
    tt = min(PEER_TOKENS, ht.shape[2])
    n1, c1, b2, e2 = _peer_topk(ht, wq_t, keys, tt)
    return _peer_dense(ht, n1, c1, b2, e2, u, vt, x1, mod, tt)


def _rope_tables(s):
    t = np.arange(s)
    pos = np.stack([t // GRID_W, t % GRID_W], axis=1).astype(np.float32)
    axis_dim = HEAD_DIM // 2
    freqs = 1.0 / (ROPE_THETA ** (jnp.arange(0, axis_dim, 2, dtype=F32) / axis_dim))
    ang = jnp.asarray(pos)[:, :, None] * freqs[None, None, :]
    cos, sin = jnp.cos(ang), jnp.sin(ang)
    zero = jnp.zeros_like(sin)
    def lanes(first_half, second_half):
        p = jnp.stack([first_half, second_half], axis=2).reshape(s, HEAD_DIM)
        return jnp.concatenate([p, p], axis=1)
    return lanes(cos, cos), lanes(-sin, zero), lanes(zero, sin)


def _block_diag_mean():
    i = np.arange(LANES)
    return jnp.asarray((i[:, None] // HEAD_DIM == i[None, :] // HEAD_DIM) / HEAD_DIM, BF16)


def kernel(x, c, ctx, c_ctx, ada_w, ada_b, norm_g, ab_w_in, ab_w_out, a_q_norm, a_k_norm, b_q_norm,
           b_k_norm, b_lambda, b_subln, c_w_in, c_w_out, c_q_norm, c_k_norm, c_rpb, peer_w_q,
           peer_keys, peer_u, peer_v):
    bsz, s, d = x.shape
    depth = ada_w.shape[0]
    assert depth == 2, "one attention-pair layer followed by one (final) neighbourhood layer"
    scale = HEAD_DIM ** -0.5

    cs = jnp.zeros((SUBLANES, d), F32).at[:bsz].set(c).at[bsz].set(c_ctx)
    mods = _ada(cs, ada_w, ada_b)
    bd = _block_diag_mean()
    rope = _rope_tables(s)

    for l in range(depth):
        last = l == depth - 1
        mod_lat = mods[l, :bsz].reshape(bsz, 6, d)
        mod_ctx = mods[l, bsz:bsz + 1].reshape(1, 6, d)
        g1 = norm_g[l, 0].reshape(1, d)
        g2 = norm_g[l, 1].reshape(1, d)
        i = l // 2
        wq_t = peer_w_q[l].T.astype(BF16)
        keys = peer_keys[l].reshape(2 * PEER_HEADS, PEER_N_KEYS, -1).astype(BF16)
        u = peer_u[l].astype(BF16)
        vt = peer_v[l].T.astype(BF16)
        if l % 2 == 0:
            lam_init = 0.8 - 0.6 * math.exp(-0.3 * l)
            qa, ka, va, qb, kb, vb = jnp.split(
                ab_w_in[i], np.cumsum([512, 128, 128, 512, 512, 512])[:-1].tolist(), axis=1)
            w = jnp.concatenate([qa, qb, ka, kb, va, vb], axis=1).astype(BF16)
            nq, nk, nv = 1024, 640, 640
            gq = jnp.concatenate([jnp.tile(a_q_norm[i], A_HEADS), jnp.tile(b_q_norm[i], 2 * B_HEADS)])
            gk = jnp.concatenate([jnp.tile(a_k_norm[i], A_KV_HEADS), jnp.tile(b_k_norm[i], 2 * B_HEADS)])
            gq = (gq * scale).reshape(1, nq)
            gk = gk.reshape(1, nk)
            q_l, kt_l, v_l = _proj(x, mod_lat, g1, w, gq, gk, bd, rope, nq, nk, nv)
            q_c, kt_c, v_c = _proj(ctx, mod_ctx, g1, w, gq, gk, bd, None, nq, nk, nv)
            lam2 = b_lambda[i]
            sub = b_subln[i].reshape(1, LANES)
            oa = _gqa(q_l, kt_l, v_l, kt_c, v_c)
            ob = _diff(q_l, kt_l, v_l, kt_c, v_c, lam2, sub, lam_init)
            w_out = ab_w_out[i].astype(BF16)
            x1, ht = _out(oa, 0, ob, 0, w_out, x, mod_lat, g2)
            if not last:
                oa_c = _gqa(q_c, None, None, kt_c, v_c)
                ob_c = _diff(q_c, None, None, kt_c, v_c, lam2, sub, lam_init)
                ctx1, ht_c = _out(oa_c, 0, ob_c, 0, w_out, ctx, mod_ctx, g2)
        else:
            w = c_w_in[i].astype(BF16)
            nq = nk = nv = C_HEADS * HEAD_DIM
            gq = (jnp.tile(c_q_norm[i], C_HEADS) * scale).reshape(1, nq)
            gk = jnp.tile(c_k_norm[i], C_HEADS).reshape(1, nk)
            q_l, kt_l, v_l = _proj(x, mod_lat, g1, w, gq, gk, bd, None, nq, nk, nv)
            q_c, kt_c, v_c = _proj(ctx, mod_ctx, g1, w, gq, gk, bd, None, nq, nk, nv)
            o = _na(q_l, kt_l, v_l, kt_c, v_c, _na_bias_table(c_rpb[i]))
            w_out = c_w_out[i].astype(BF16)
            x1, ht = _out(o, 0, o, 1, w_out, x, mod_lat, g2)
        x = _peer(ht, x1, mod_lat, wq_t, keys, u, vt)
        if not last:
            ctx = _peer(ht_c, ctx1, mod_ctx, wq_t, keys, u, vt)
    return x
```

```python
import functools
import math

import numpy as np
import jax
import jax.numpy as jnp
from jax import lax
from jax.experimental import pallas as pl
from jax.experimental.pallas import tpu as pltpu

F32 = jnp.float32
BF16 = jnp.bfloat16

D_MODEL = 1024
GRID_W = 64
HEAD_DIM = 64
ROPE_THETA = 10000.0
EPS = 1e-6
A_HEADS = 8
A_KV_HEADS = 2
B_HEADS = 4
C_HEADS = 16
NA_ROWS = 8
NA_COLS = 16
PEER_HEADS = 8
PEER_N_KEYS = 128
PEER_TOPK = 16

LANES = 128
SUBLANES = 8
BF16_ROWS = 16
VMEM_LIMIT = 56 * 1024 * 1024
NEG = -1e30

ROW_TILE = 256
Q_TILE = 256
K_CHUNK = 512
NA_BLOCK_ROWS = 4
PEER_TOKENS = 512
PEER_EXPERTS = 1024


def _params(*sem):
    return pltpu.CompilerParams(dimension_semantics=sem, vmem_limit_bytes=VMEM_LIMIT)


def _ada_kernel(cs_ref, w_ref, b_ref, o_ref):
    cs = cs_ref[...]
    a = cs * jax.nn.sigmoid(cs)
    o_ref[0] = jnp.dot(a, w_ref[0], preferred_element_type=F32,
                       precision=lax.Precision.HIGHEST) + b_ref[0]


def _ada(cs, ada_w, ada_b):
    depth, d, n = ada_w.shape
    tn = 1536
    return pl.pallas_call(
        _ada_kernel,
        grid=(depth, n // tn),
        in_specs=[pl.BlockSpec((SUBLANES, d), lambda l, j: (0, 0)),
                  pl.BlockSpec((1, d, tn), lambda l, j: (l, 0, j)),
                  pl.BlockSpec((1, 1, tn), lambda l, j: (l, 0, j))],
        out_specs=pl.BlockSpec((1, SUBLANES, tn), lambda l, j: (l, 0, j)),
        out_shape=jax.ShapeDtypeStruct((depth, SUBLANES, n), F32),
        compiler_params=_params("parallel", "parallel"),
        name="ada_mod",
    )(cs, ada_w, ada_b.reshape(depth, 1, n))


def _modulated_norm(x, gain, shift, scale):
    ms = jnp.mean(x * x, axis=-1, keepdims=True)
    return (x * lax.rsqrt(ms + EPS)) * gain * (1.0 + scale) + shift


def _proj_kernel(*refs, nq, nk, rope):
    if rope:
        (x_ref, mod_ref, g_ref, w_ref, gq_ref, gk_ref, bd_ref, cos_ref, sa_ref, sb_ref,
         q_ref, kt_ref, v_ref) = refs
    else:
        x_ref, mod_ref, g_ref, w_ref, gq_ref, gk_ref, bd_ref, q_ref, kt_ref, v_ref = refs
    h = _modulated_norm(x_ref[0], g_ref[...], mod_ref[0, 0:1, :], mod_ref[0, 1:2, :])
    p = jnp.dot(h.astype(BF16), w_ref[...], preferred_element_type=F32)
    bd = bd_ref[...]

    def head_norm(c0, gain):
        yc = p[:, c0:c0 + LANES]
        ss = jnp.dot((yc * yc).astype(BF16), bd, preferred_element_type=F32)
        yn = yc * lax.rsqrt(ss + EPS) * gain
        if rope:
            yn = (yn * cos_ref[...] + pltpu.roll(yn, LANES - 16, 1) * sa_ref[...]
                  + pltpu.roll(yn, 16, 1) * sb_ref[...])
        return yn

    for c in range(nq // LANES):
        c0 = c * LANES
        q_ref[0, :, c0:c0 + LANES] = head_norm(c0, gq_ref[:, c0:c0 + LANES]).astype(BF16)
    for c in range(nk // LANES):
        c0 = c * LANES
        yn = head_norm(nq + c0, gk_ref[:, c0:c0 + LANES])
        kt_ref[0, c0:c0 + LANES, :] = yn.T.astype(BF16)
    v_ref[0] = p[:, nq + nk:].astype(BF16)


def _proj(x, mod, gain, w, gq, gk, bd, rope_tabs, nq, nk, nv):
    b, l, d = x.shape
    tm = ROW_TILE
    n = nq + nk + nv
    mod_map = (lambda i, j: (i, 0, 0)) if mod.shape[0] > 1 else (lambda i, j: (0, 0, 0))
    in_specs = [pl.BlockSpec((1, tm, d), lambda i, j: (i, j, 0)),
                pl.BlockSpec((1, 6, d), mod_map),
                pl.BlockSpec((1, d), lambda i, j: (0, 0)),
                pl.BlockSpec((d, n), lambda i, j: (0, 0)),
                pl.BlockSpec((1, nq), lambda i, j: (0, 0)),
                pl.BlockSpec((1, nk), lambda i, j: (0, 0)),
                pl.BlockSpec((LANES, LANES), lambda i, j: (0, 0))]
    args = [x, mod, gain, w, gq, gk, bd]
    if rope_tabs is not None:
        in_specs += [pl.BlockSpec((tm, LANES), lambda i, j: (j, 0))] * 3
        args += list(rope_tabs)
    return pl.pallas_call(
        functools.partial(_proj_kernel, nq=nq, nk=nk, rope=rope_tabs is not None),
        grid=(b, l // tm),
        in_specs=in_specs,
        out_specs=[pl.BlockSpec((1, tm, nq), lambda i, j: (i, j, 0)),
                   pl.BlockSpec((1, nk, tm), lambda i, j: (i, 0, j)),
                   pl.BlockSpec((1, tm, nv), lambda i, j: (i, j, 0))],
        out_shape=[jax.ShapeDtypeStruct((b, l, nq), BF16),
                   jax.ShapeDtypeStruct((b, nk, l), BF16),
                   jax.ShapeDtypeStruct((b, l, nv), BF16)],
        compiler_params=_params("parallel", "parallel"),
        name="in_proj",
    )(*args)


def _flash_init(m_ref, l_ref, acc_ref):
    m_ref[...] = jnp.full(m_ref.shape, -jnp.inf, F32)
    l_ref[...] = jnp.zeros(l_ref.shape, F32)
    acc_ref[...] = jnp.zeros(acc_ref.shape, F32)


def _flash_step(i, q, kt, v, m_ref, l_ref, acc_ref):
    s = jnp.dot(q, kt, preferred_element_type=F32)
    m_prev = m_ref[i]
    m_new = jnp.maximum(m_prev, jnp.max(s, axis=-1, keepdims=True))
    alpha = jnp.exp(m_prev - m_new)
    p = jnp.exp(s - m_new)
    l_ref[i] = alpha * l_ref[i] + jnp.sum(p, axis=-1, keepdims=True)
    acc_ref[i] = alpha * acc_ref[i] + jnp.dot(p.astype(BF16), v, preferred_element_type=F32)
    m_ref[i] = m_new


def _flash_sweep(q_ref, ktl_ref, vl_ref, ktc_ref, vc_ref, m_ref, l_ref, acc_ref, n_maps, kt_row):
    _flash_init(m_ref, l_ref, acc_ref)

    def maps(kt, v):
        for i in range(n_maps):
            q = q_ref[0, :, i * HEAD_DIM:(i + 1) * HEAD_DIM]
            r0 = kt_row(i)
            _flash_step(i, q, kt[r0:r0 + HEAD_DIM, :], v, m_ref, l_ref, acc_ref)

    if ktl_ref is not None:
        s_len = ktl_ref.shape[2]
        tk = min(K_CHUNK, s_len)

        def body(c, carry):
            off = pl.multiple_of(c * tk, tk)
            maps(ktl_ref[0, :, pl.ds(off, tk)], vl_ref[0, pl.ds(off, tk), :])
            return carry

        lax.fori_loop(0, s_len // tk, body, 0)
    maps(ktc_ref[0], vc_ref[0])


def _gqa_kernel(*refs, has_lat):
    if has_lat:
        q_ref, ktl_ref, vl_ref, ktc_ref, vc_ref, o_ref, m_ref, l_ref, acc_ref = refs
    else:
        q_ref, ktc_ref, vc_ref, o_ref, m_ref, l_ref, acc_ref = refs
        ktl_ref = vl_ref = None
    group = A_HEADS // A_KV_HEADS
    _flash_sweep(q_ref, ktl_ref, vl_ref, ktc_ref, vc_ref, m_ref, l_ref, acc_ref, A_HEADS,
                 lambda i: (i // group) * HEAD_DIM)
    lane = lax.broadcasted_iota(jnp.int32, (q_ref.shape[1], LANES), 1)
    for j in range(A_HEADS // 2):
        g = (2 * j) // group
        a = acc_ref[2 * j] / l_ref[2 * j]
        c = acc_ref[2 * j + 1] / l_ref[2 * j + 1]
        if g == 0:
            c = pltpu.roll(c, HEAD_DIM, 1)
        else:
            a = pltpu.roll(a, HEAD_DIM, 1)
        o_ref[0, :, j * LANES:(j + 1) * LANES] = jnp.where(lane < HEAD_DIM, a, c).astype(BF16)


def _diff_kernel(*refs, has_lat, lam_init):
    if has_lat:
        q_ref, ktl_ref, vl_ref, ktc_ref, vc_ref, lam_ref, sub_ref, o_ref, m_ref, l_ref, acc_ref = refs
    else:
        q_ref, ktc_ref, vc_ref, lam_ref, sub_ref, o_ref, m_ref, l_ref, acc_ref = refs
        ktl_ref = vl_ref = None
    _flash_sweep(q_ref, ktl_ref, vl_ref, ktc_ref, vc_ref, m_ref, l_ref, acc_ref, 2,
                 lambda i: i * HEAD_DIM)
    lf = lam_ref[...]
    lam = (jnp.exp(jnp.sum(lf[0:1] * lf[1:2], axis=-1, keepdims=True))
           - jnp.exp(jnp.sum(lf[2:3] * lf[3:4], axis=-1, keepdims=True)) + lam_init)
    o = acc_ref[0] / l_ref[0] - lam * (acc_ref[1] / l_ref[1])
    ms = jnp.mean(o * o, axis=-1, keepdims=True)
    o = (o * lax.rsqrt(ms + EPS)) * sub_ref[...] * (1.0 - lam_init)
    o_ref[0] = o.astype(BF16)


def _flash_scratch(n_maps, tq):
    return [pltpu.VMEM((n_maps, tq, 1), F32), pltpu.VMEM((n_maps, tq, 1), F32),
            pltpu.VMEM((n_maps, tq, LANES), F32)]


def _gqa(q, kt_lat, v_lat, kt_ctx, v_ctx):
    b, lq, _ = q.shape
    tq = min(Q_TILE, lq)
    nc = kt_ctx.shape[2]
    has_lat = kt_lat is not None
    in_specs = [pl.BlockSpec((1, tq, A_HEADS * HEAD_DIM), lambda i, j: (i, j, 0))]
    args = [q]
    if has_lat:
        s = kt_lat.shape[2]
        in_specs += [pl.BlockSpec((1, LANES, s), lambda i, j: (i, 0, 0)),
                     pl.BlockSpec((1, s, LANES), lambda i, j: (i, 0, 0))]
        args += [kt_lat, v_lat]
    in_specs += [pl.BlockSpec((1, LANES, nc), lambda i, j: (i, 0, 0)),
                 pl.BlockSpec((1, nc, LANES), lambda i, j: (i, 0, 0))]
    args += [kt_ctx, v_ctx]
    return pl.pallas_call(
        functools.partial(_gqa_kernel, has_lat=has_lat),
        grid=(b, lq // tq),
        in_specs=in_specs,
        out_specs=pl.BlockSpec((1, tq, A_HEADS * HEAD_DIM), lambda i, j: (i, j, 0)),
        out_shape=jax.ShapeDtypeStruct((b, lq, A_HEADS * HEAD_DIM), BF16),
        scratch_shapes=_flash_scratch(A_HEADS, tq),
        compiler_params=_params("parallel", "parallel"),
        name="gqa_attn",
    )(*args)


def _diff(q, kt_lat, v_lat, kt_ctx, v_ctx, b_lambda, b_subln, lam_init):
    b, lq, _ = q.shape
    tq = min(Q_TILE, lq)
    nc = kt_ctx.shape[2]
    has_lat = kt_lat is not None
    q0 = A_HEADS * HEAD_DIM // LANES
    in_specs = [pl.BlockSpec((1, tq, LANES), lambda i, h, j: (i, j, q0 + h))]
    args = [q]
    if has_lat:
        s = kt_lat.shape[2]
        in_specs += [pl.BlockSpec((1, LANES, s), lambda i, h, j: (i, 1 + h, 0)),
                     pl.BlockSpec((1, s, LANES), lambda i, h, j: (i, 0, 1 + h))]
        args += [kt_lat, v_lat]
    in_specs += [pl.BlockSpec((1, LANES, nc), lambda i, h, j: (i, 1 + h, 0)),
                 pl.BlockSpec((1, nc, LANES), lambda i, h, j: (i, 0, 1 + h)),
                 pl.BlockSpec((4, HEAD_DIM), lambda i, h, j: (0, 0)),
                 pl.BlockSpec((1, LANES), lambda i, h, j: (0, 0))]
    args += [kt_ctx, v_ctx, b_lambda, b_subln]
    return pl.pallas_call(
        functools.partial(_diff_kernel, has_lat=has_lat, lam_init=lam_init),
        grid=(b, B_HEADS, lq // tq),
        in_specs=in_specs,
        out_specs=pl.BlockSpec((1, tq, LANES), lambda i, h, j: (i, j, h)),
        out_shape=jax.ShapeDtypeStruct((b, lq, B_HEADS * LANES), BF16),
        scratch_shapes=_flash_scratch(2, tq),
        compiler_params=_params("parallel", "parallel", "parallel"),
        name="diff_attn",
    )(*args)


def _na_kernel(q_ref, km_ref, k0_ref, kp_ref, kc_ref, vm_ref, v0_ref, vp_ref, vc_ref, bias_ref,
               o_ref, *, rows):
    i = pl.program_id(1)
    tq = q_ref.shape[1]
    nb = 3 * tq
    qrow = NA_BLOCK_ROWS * i + lax.broadcasted_iota(jnp.int32, (tq, nb), 0) // GRID_W
    krow = NA_BLOCK_ROWS * (i - 1) + lax.broadcasted_iota(jnp.int32, (tq, nb), 1) // GRID_W
    rs = jnp.clip(qrow - NA_ROWS // 2, 0, rows - NA_ROWS)
    row_mask = jnp.where((krow >= rs) & (krow < rs + NA_ROWS), 0.0, NEG).astype(F32)
    lane = lax.broadcasted_iota(jnp.int32, (tq, LANES), 1)
    kts = (km_ref, k0_ref, kp_ref)
    vs = (vm_ref, v0_ref, vp_ref)
    for pair in range(C_HEADS // 2):
        outs = []
        for h in (2 * pair, 2 * pair + 1):
            r0 = h * HEAD_DIM
            q = q_ref[0, :, r0:r0 + HEAD_DIM]
            s_band = jnp.concatenate(
                [jnp.dot(q, k[0, r0:r0 + HEAD_DIM, :], preferred_element_type=F32) for k in kts],
                axis=1) + bias_ref[h] + row_mask
            s_ctx = jnp.dot(q, kc_ref[0, r0:r0 + HEAD_DIM, :], preferred_element_type=F32)
            m = jnp.maximum(jnp.max(s_band, axis=-1, keepdims=True),
                            jnp.max(s_ctx, axis=-1, keepdims=True))
            p_band = jnp.exp(s_band - m)
            p_ctx = jnp.exp(s_ctx - m)
            denom = jnp.sum(p_band, axis=-1, keepdims=True) + jnp.sum(p_ctx, axis=-1, keepdims=True)
            c0 = pair * LANES
            pv = jnp.dot(p_ctx.astype(BF16), vc_ref[0, :, c0:c0 + LANES], preferred_element_type=F32)
            for t, v in enumerate(vs):
                pv += jnp.dot(p_band[:, t * tq:(t + 1) * tq].astype(BF16), v[0, :, c0:c0 + LANES],
                              preferred_element_type=F32)
            outs.append(pv / denom)
        a, c = outs
        o_ref[0, :, pair * LANES:(pair + 1) * LANES] = jnp.where(lane < HEAD_DIM, a, c).astype(BF16)


def _na(q, kt_lat, v_lat, kt_ctx, v_ctx, bias):
    b, s, w = q.shape
    tq = NA_BLOCK_ROWS * GRID_W
    nblk = s // tq
    nc = kt_ctx.shape[2]
    rows = s // GRID_W

    def kt_spec(d):
        return pl.BlockSpec((1, w, tq), lambda i, j: (i, 0, jnp.clip(j + d, 0, nblk - 1)))

    def v_spec(d):
        return pl.BlockSpec((1, tq, w), lambda i, j: (i, jnp.clip(j + d, 0, nblk - 1), 0))

    return pl.pallas_call(
        functools.partial(_na_kernel, rows=rows),
        grid=(b, nblk),
        in_specs=[pl.BlockSpec((1, tq, w), lambda i, j: (i, j, 0)),
                  kt_spec(-1), kt_spec(0), kt_spec(1),
                  pl.BlockSpec((1, w, nc), lambda i, j: (i, 0, 0)),
                  v_spec(-1), v_spec(0), v_spec(1),
                  pl.BlockSpec((1, nc, w), lambda i, j: (i, 0, 0)),
                  pl.BlockSpec(bias.shape, lambda i, j: (0, 0, 0))],
        out_specs=pl.BlockSpec((1, tq, w), lambda i, j: (i, j, 0)),
        out_shape=jax.ShapeDtypeStruct((b, s, w), BF16),
        compiler_params=_params("parallel", "parallel"),
        name="na_attn",
    )(q, kt_lat, kt_lat, kt_lat, kt_ctx, v_lat, v_lat, v_lat, v_ctx, bias)


def _na_bias_table(rpb):
    heads = rpb.shape[0]
    nr, nc = 2 * NA_ROWS - 1, 2 * NA_COLS - 1
    j = np.arange(GRID_W)
    dc = j[None, :] - j[:, None] + (NA_COLS - 1)
    cs = np.clip(j - NA_COLS // 2, 0, GRID_W - NA_COLS)
    col_ok = (j[None, :] >= cs[:, None]) & (j[None, :] < cs[:, None] + NA_COLS)
    onehot = (np.arange(nc)[:, None, None] == dc[None]) & col_ok[None]
    cols = jnp.einsum('hrm,mjk->hrjk', rpb, jnp.asarray(onehot, F32),
                      precision=lax.Precision.HIGHEST)
    cols = jnp.where(col_ok[None, None], cols, NEG)
    band = 3 * NA_BLOCK_ROWS
    per_q = [cols[:, NA_ROWS - 1 - NA_BLOCK_ROWS - qr:NA_ROWS - 1 - NA_BLOCK_ROWS - qr + band]
             for qr in range(NA_BLOCK_ROWS)]
    t = jnp.stack(per_q, axis=1)
    assert t.shape == (heads, NA_BLOCK_ROWS, band, GRID_W, GRID_W) and nr == band + NA_BLOCK_ROWS - 1
    return t.transpose(0, 1, 3, 2, 4).reshape(heads, NA_BLOCK_ROWS * GRID_W, band * GRID_W)


def _out_kernel(o1_ref, o2_ref, w_ref, x_ref, mod_ref, g_ref, x1_ref, ht_ref):
    half = o1_ref.shape[2]
    o = (jnp.dot(o1_ref[0], w_ref[0:half, :], preferred_element_type=F32)
         + jnp.dot(o2_ref[0], w_ref[half:, :], preferred_element_type=F32))
    x1 = x_ref[0] + mod_ref[0, 2:3, :] * o
    x1_ref[0] = x1
    h2 = _modulated_norm(x1, g_ref[...], mod_ref[0, 3:4, :], mod_ref[0, 4:5, :])
    ht_ref[0] = h2.T.astype(BF16)


def _out(o1, o1_blk, o2, o2_blk, w, x, mod, gain):
    b, l, d = x.shape
    tm = min(ROW_TILE, l)
    half = d // 2
    mod_map = (lambda i, j: (i, 0, 0)) if mod.shape[0] > 1 else (lambda i, j: (0, 0, 0))
    return pl.pallas_call(
        _out_kernel,
        grid=(b, l // tm),
        in_specs=[pl.BlockSpec((1, tm, half), lambda i, j: (i, j, o1_blk)),
                  pl.BlockSpec((1, tm, half), lambda i, j: (i, j, o2_blk)),
                  pl.BlockSpec((d, d), lambda i, j: (0, 0)),
                  pl.BlockSpec((1, tm, d), lambda i, j: (i, j, 0)),
                  pl.BlockSpec((1, 6, d), mod_map),
                  pl.BlockSpec((1, d), lambda i, j: (0, 0))],
        out_specs=[pl.BlockSpec((1, tm, d), lambda i, j: (i, j, 0)),
                   pl.BlockSpec((1, d, tm), lambda i, j: (i, 0, j))],
        out_shape=[jax.ShapeDtypeStruct((b, l, d), F32),
                   jax.ShapeDtypeStruct((b, d, l), BF16)],
        compiler_params=_params("parallel", "parallel"),
        name="out_proj",
    )(o1, o2, w, x, mod, gain)


def _top16(s, iota, exact):
    rank = jnp.full(s.shape, float(PEER_TOPK), F32)
    big = float(s.shape[0])
    vals = []
    for r in range(PEER_TOPK):
        m = jnp.max(s, axis=0, keepdims=True)
        hit = s == m
        if exact:
            hit = iota == jnp.min(jnp.where(hit, iota, big), axis=0, keepdims=True)
        rank = jnp.where(hit, float(r), rank)
        s = jnp.where(hit, -jnp.inf, s)
        vals.append(m)
    return vals, rank


def _rows_tile(rows, iota8):
    t = jnp.zeros((SUBLANES,) + rows[0].shape[1:], F32)
    for k, r in enumerate(rows):
        t = jnp.where(iota8 == k, r, t)
    return t


_CAND_COUNTS = [PEER_TOPK // (a + 1) for a in range(SUBLANES)]


def _retrieve(s1, s2, exact):
    iota = lax.broadcasted_iota(jnp.int32, (PEER_N_KEYS, LANES), 0).astype(F32)
    iota8 = lax.broadcasted_iota(jnp.int32, (SUBLANES, LANES), 0)
    iota_c = lax.broadcasted_iota(jnp.int32, (SUBLANES * (SUBLANES + 2), LANES), 0).astype(F32)
    v1, r1 = _top16(s1, iota, exact)
    v2, r2 = _top16(s2, iota, exact)
    e1 = [jnp.exp(v - v1[0]) for v in v1]
    e2 = [jnp.exp(v - v2[0]) for v in v2]
    v2_lo, v2_hi = _rows_tile(v2[:8], iota8), _rows_tile(v2[8:], iota8)
    e2_lo, e2_hi = _rows_tile(e2[:8], iota8), _rows_tile(e2[8:], iota8)
    v1_hi, e1_hi = _rows_tile(v1[8:], iota8), _rows_tile(e1[8:], iota8)
    cand, ecand = [], []
    for a, cnt in enumerate(_CAND_COUNTS):
        cand.append(jnp.where(iota8 < cnt, v1[a] + v2_lo, -jnp.inf))
        ecand.append(e1[a] * e2_lo)
        if a == 0:
            cand.append(v1[0] + v2_hi)
            ecand.append(e1[0] * e2_hi)
    cand.append(v1_hi + v2[0])
    ecand.append(e1_hi * e2[0])
    cand = jnp.concatenate(cand, axis=0)
    ecand = jnp.concatenate(ecand, axis=0)
    _, rc = _top16(cand, iota_c, exact)
    sel = (rc < float(PEER_TOPK)).astype(F32)
    ranked = (jnp.sum(sel, axis=0, keepdims=True)
              + jnp.sum((r1 < float(PEER_TOPK)).astype(F32), axis=0, keepdims=True)
              + jnp.sum((r2 < float(PEER_TOPK)).astype(F32), axis=0, keepdims=True))
    z = jnp.sum(sel * ecand, axis=0, keepdims=True)
    n_rows = [jnp.sum(sel[0:2 * SUBLANES], axis=0, keepdims=True)]
    for a in range(1, SUBLANES):
        n_rows.append(jnp.sum(sel[SUBLANES * (a + 1):SUBLANES * (a + 2)], axis=0, keepdims=True))
    n_hi = sel[SUBLANES * (SUBLANES + 1):]
    n1 = jnp.zeros(s1.shape, F32)
    for a in range(PEER_TOPK):
        na = n_rows[a] if a < SUBLANES else n_hi[a - SUBLANES:a - SUBLANES + 1]
        n1 = jnp.where(r1 == float(a), na, n1)
    return n1, jnp.exp(s1 - v1[0]) / z, r2, jnp.exp(s2 - v2[0]), ranked


def _topk_kernel(ht_ref, wq_ref, keys_ref, n1_ref, c1_ref, b2_ref, e2_ref, qt_ref, s_ref):
    chunks = ht_ref.shape[2] // LANES
    qt_ref[...] = jnp.dot(wq_ref[...], ht_ref[0], preferred_element_type=F32).astype(BF16)
    for hp in range(2 * PEER_HEADS):
        s_ref[hp] = jnp.dot(keys_ref[hp], qt_ref[hp * PEER_N_KEYS:(hp + 1) * PEER_N_KEYS, :],
                            preferred_element_type=F32)

    def unit(i, carry):
        h = i // chunks
        sl = pl.ds(pl.multiple_of((i % chunks) * LANES, LANES), LANES)
        s1 = s_ref[2 * h, :, sl]
        s2 = s_ref[2 * h + 1, :, sl]

        def emit(exact):
            n1, c1, b2, e2, ranked = _retrieve(s1, s2, exact)
            n1_ref[0, h, :, sl] = n1
            c1_ref[0, h, :, sl] = c1
            b2_ref[0, h, :, sl] = b2.astype(BF16)
            e2_ref[0, h, :, sl] = e2.astype(BF16)
            return ranked

        ranked = emit(False)

        @pl.when(jnp.max(ranked) > float(3 * PEER_TOPK))
        def _():
            emit(True)

        return carry

    lax.fori_loop(0, PEER_HEADS * chunks, unit, 0)


def _peer_topk(ht, wq_t, keys, tt):
    b, d, l = ht.shape
    nqd = wq_t.shape[0]
    shp = (b, PEER_HEADS, PEER_N_KEYS, l)
    ospec = pl.BlockSpec((1, PEER_HEADS, PEER_N_KEYS, tt), lambda i, j: (i, 0, 0, j))
    return pl.pallas_call(
        _topk_kernel,
        grid=(b, l // tt),
        in_specs=[pl.BlockSpec((1, d, tt), lambda i, j: (i, 0, j)),
                  pl.BlockSpec((nqd, d), lambda i, j: (0, 0)),
                  pl.BlockSpec(keys.shape, lambda i, j: (0, 0, 0))],
        out_specs=[ospec] * 4,
        out_shape=[jax.ShapeDtypeStruct(shp, F32), jax.ShapeDtypeStruct(shp, F32),
                   jax.ShapeDtypeStruct(shp, BF16), jax.ShapeDtypeStruct(shp, BF16)],
        scratch_shapes=[pltpu.VMEM((nqd, tt), BF16),
                        pltpu.VMEM((2 * PEER_HEADS, PEER_N_KEYS, tt), F32)],
        compiler_params=_params("parallel", "parallel"),
        name="peer_topk",
    )(ht, wq_t, keys)


def _peer_kernel(ht_ref, n1_ref, c1_ref, b2_ref, e2_ref, u_ref, vt_ref, x_ref, mod_ref, o_ref,
                 a_ref, g_ref, acc_ref):
    e = pl.program_id(2)
    et = u_ref.shape[0]

    @pl.when(e == 0)
    def _():
        acc_ref[...] = jnp.zeros(acc_ref.shape, F32)

    a_ref[...] = jnp.dot(u_ref[...], ht_ref[0], preferred_element_type=F32)
    tile3 = (PEER_N_KEYS // BF16_ROWS, BF16_ROWS, LANES)
    for k in range(et // PEER_N_KEYS):
        rows = slice(k * PEER_N_KEYS, (k + 1) * PEER_N_KEYS)
        for c in range(ht_ref.shape[2] // LANES):
            sl = slice(c * LANES, (c + 1) * LANES)
            w = jnp.zeros(tile3, BF16)
            for h in range(PEER_HEADS):
                n_t = jnp.broadcast_to(n1_ref[0, h, k:k + 1, sl], tile3[1:]).astype(BF16)[None]
                c_t = jnp.broadcast_to(c1_ref[0, h, k:k + 1, sl], tile3[1:]).astype(BF16)[None]
                e2 = e2_ref[0, h, :, :, sl]
                w = w + jnp.where(b2_ref[0, h, :, :, sl] < n_t, e2, jnp.zeros_like(e2)) * c_t
            act = jax.nn.gelu(a_ref[rows, sl]).astype(BF16)
            g_ref[rows, sl] = (w * act.reshape(tile3)).reshape(PEER_N_KEYS, LANES)
    acc_ref[...] += jnp.dot(vt_ref[...], g_ref[...], preferred_element_type=F32)

    @pl.when(e == pl.num_programs(2) - 1)
    def _():
        o_ref[0] = x_ref[0] + mod_ref[0, 5:6, :] * acc_ref[...].T


def _peer_dense(ht, n1, c1, b2, e2, u, vt, x, mod, tt):
    b, d, l = ht.shape
    ne = u.shape[0]
    et = PEER_EXPERTS
    mod_map = (lambda i, j, k: (i, 0, 0)) if mod.shape[0] > 1 else (lambda i, j, k: (0, 0, 0))
    sel_spec = pl.BlockSpec((1, PEER_HEADS, PEER_N_KEYS // BF16_ROWS, BF16_ROWS, tt),
                            lambda i, j, k: (i, 0, 0, 0, j))
    row_spec = pl.BlockSpec((1, PEER_HEADS, et // PEER_N_KEYS, tt), lambda i, j, k: (i, 0, k, j))
    return pl.pallas_call(
        _peer_kernel,
        grid=(b, l // tt, ne // et),
        in_specs=[pl.BlockSpec((1, d, tt), lambda i, j, k: (i, 0, j)),
                  row_spec, row_spec, sel_spec, sel_spec,
                  pl.BlockSpec((et, d), lambda i, j, k: (k, 0)),
                  pl.BlockSpec((d, et), lambda i, j, k: (0, k)),
                  pl.BlockSpec((1, tt, d), lambda i, j, k: (i, j, 0)),
                  pl.BlockSpec((1, 6, d), mod_map)],
        out_specs=pl.BlockSpec((1, tt, d), lambda i, j, k: (i, j, 0)),
        out_shape=jax.ShapeDtypeStruct((b, l, d), F32),
        scratch_shapes=[pltpu.VMEM((et, tt), F32), pltpu.VMEM((et, tt), BF16),
                        pltpu.VMEM((d, tt), F32)],
        compiler_params=_params("parallel", "parallel", "arbitrary"),
        name="peer_mix",
    )(ht, n1, c1, b2, e2, u, vt, x, mod)


def _peer(ht, x1, mod, wq_t, keys, u, vt):
    tt = min(PEER_TOKENS, ht.shape[2])
    n1, c1, b2, e2 = _peer_topk(ht, wq_t, keys, tt)
    packed = (ht.shape[0], PEER_HEADS, PEER_N_KEYS // BF16_ROWS, BF16_ROWS, ht.shape[2])
    return _peer_dense(ht, n1, c1, b2.reshape(packed), e2.reshape(packed), u, vt, x1, mod, tt)


def _rope_tables(s):
    t = np.arange(s)
    pos = np.stack([t // GRID_W, t % GRID_W], axis=1).astype(np.float32)
    axis_dim = HEAD_DIM // 2
    freqs = 1.0 / (ROPE_THETA ** (jnp.arange(0, axis_dim, 2, dtype=F32) / axis_dim))
    ang = jnp.asarray(pos)[:, :, None] * freqs[None, None, :]
    cos, sin = jnp.cos(ang), jnp.sin(ang)
    zero = jnp.zeros_like(sin)
    def lanes(first_half, second_half):
        p = jnp.stack([first_half, second_half], axis=2).reshape(s, HEAD_DIM)
        return jnp.concatenate([p, p], axis=1)
    return lanes(cos, cos), lanes(-sin, zero), lanes(zero, sin)


def _block_diag_mean():
    i = np.arange(LANES)
    return jnp.asarray((i[:, None] // HEAD_DIM == i[None, :] // HEAD_DIM) / HEAD_DIM, BF16)


def kernel(x, c, ctx, c_ctx, ada_w, ada_b, norm_g, ab_w_in, ab_w_out, a_q_norm, a_k_norm, b_q_norm,
           b_k_norm, b_lambda, b_subln, c_w_in, c_w_out, c_q_norm, c_k_norm, c_rpb, peer_w_q,
           peer_keys, peer_u, peer_v):
    bsz, s, d = x.shape
    depth = ada_w.shape[0]
    assert depth == 2, "one attention-pair layer followed by one (final) neighbourhood layer"
    scale = HEAD_DIM ** -0.5

    cs = jnp.zeros((SUBLANES, d), F32).at[:bsz].set(c).at[bsz].set(c_ctx)
    mods = _ada(cs, ada_w, ada_b)
    bd = _block_diag_mean()
    rope = _rope_tables(s)

    for l in range(depth):
        last = l == depth - 1
        mod_lat = mods[l, :bsz].reshape(bsz, 6, d)
        mod_ctx = mods[l, bsz:bsz + 1].reshape(1, 6, d)
        g1 = norm_g[l, 0].reshape(1, d)
        g2 = norm_g[l, 1].reshape(1, d)
        i = l // 2
        wq_t = peer_w_q[l].T.astype(BF16)
        keys = peer_keys[l].reshape(2 * PEER_HEADS, PEER_N_KEYS, -1).astype(BF16)
        u = peer_u[l].astype(BF16)
        vt = peer_v[l].T.astype(BF16)
        if l % 2 == 0:
            lam_init = 0.8 - 0.6 * math.exp(-0.3 * l)
            qa, ka, va, qb, kb, vb = jnp.split(
                ab_w_in[i], np.cumsum([512, 128, 128, 512, 512, 512])[:-1].tolist(), axis=1)
            w = jnp.concatenate([qa, qb, ka, kb, va, vb], axis=1).astype(BF16)
            nq, nk, nv = 1024, 640, 640
            gq = jnp.concatenate([jnp.tile(a_q_norm[i], A_HEADS), jnp.tile(b_q_norm[i], 2 * B_HEADS)])
            gk = jnp.concatenate([jnp.tile(a_k_norm[i], A_KV_HEADS), jnp.tile(b_k_norm[i], 2 * B_HEADS)])
            gq = (gq * scale).reshape(1, nq)
            gk = gk.reshape(1, nk)
            q_l, kt_l, v_l = _proj(x, mod_lat, g1, w, gq, gk, bd, rope, nq, nk, nv)
            q_c, kt_c, v_c = _proj(ctx, mod_ctx, g1, w, gq, gk, bd, None, nq, nk, nv)
            lam2 = b_lambda[i]
            sub = b_subln[i].reshape(1, LANES)
            oa = _gqa(q_l, kt_l, v_l, kt_c, v_c)
            ob = _diff(q_l, kt_l, v_l, kt_c, v_c, lam2, sub, lam_init)
            w_out = ab_w_out[i].astype(BF16)
            x1, ht = _out(oa, 0, ob, 0, w_out, x, mod_lat, g2)
            if not last:
                oa_c = _gqa(q_c, None, None, kt_c, v_c)
                ob_c = _diff(q_c, None, None, kt_c, v_c, lam2, sub, lam_init)
                ctx1, ht_c = _out(oa_c, 0, ob_c, 0, w_out, ctx, mod_ctx, g2)
        else:
            w = c_w_in[i].astype(BF16)
            nq = nk = nv = C_HEADS * HEAD_DIM
            gq = (jnp.tile(c_q_norm[i], C_HEADS) * scale).reshape(1, nq)
            gk = jnp.tile(c_k_norm[i], C_HEADS).reshape(1, nk)
            q_l, kt_l, v_l = _proj(x, mod_lat, g1, w, gq, gk, bd, None, nq, nk, nv)
            q_c, kt_c, v_c = _proj(ctx, mod_ctx, g1, w, gq, gk, bd, None, nq, nk, nv)
            o = _na(q_l, kt_l, v_l, kt_c, v_c, _na_bias_table(c_rpb[i]))
            w_out = c_w_out[i].astype(BF16)
            x1, ht = _out(o, 0, o, 1, w_out, x, mod_lat, g2)
        x = _peer(ht, x1, mod_lat, wq_t, keys, u, vt)
        if not last:
            ctx = _peer(ht_c, ctx1, mod_ctx, wq_t, keys, u, vt)
    return x
```

```python
import functools
import math

import numpy as np
import jax
import jax.numpy as jnp
from jax import lax
from jax.experimental import pallas as pl
from jax.experimental.pallas import tpu as pltpu

F32 = jnp.float32
BF16 = jnp.bfloat16

D_MODEL = 1024
GRID_W = 64
HEAD_DIM = 64
ROPE_THETA = 10000.0
EPS = 1e-6
A_HEADS = 8
A_KV_HEADS = 2
B_HEADS = 4
C_HEADS = 16
NA_ROWS = 8
NA_COLS = 16
PEER_HEADS = 8
PEER_N_KEYS = 128
PEER_TOPK = 16

LANES = 128
SUBLANES = 8
BF16_ROWS = 16
VMEM_LIMIT = 56 * 1024 * 1024
NEG = -1e30

ROW_TILE = 256
Q_TILE = 256
K_CHUNK = 1024
NA_BLOCK_ROWS = 4
PEER_TOKENS = 512
PEER_EXPERTS = 1024
_GELU_K = math.sqrt(2.0 / math.pi)
_LOG2E = math.log2(math.e)


def _params(*sem):
    return pltpu.CompilerParams(dimension_semantics=sem, vmem_limit_bytes=VMEM_LIMIT)


def _ada_kernel(cs_ref, w_ref, b_ref, o_ref):
    cs = cs_ref[...]
    a = cs * jax.nn.sigmoid(cs)
    o_ref[0] = jnp.dot(a, w_ref[0], preferred_element_type=F32,
                       precision=lax.Precision.HIGHEST) + b_ref[0]


def _ada(cs, ada_w, ada_b):
    depth, d, n = ada_w.shape
    tn = 1536
    return pl.pallas_call(
        _ada_kernel,
        grid=(depth, n // tn),
        in_specs=[pl.BlockSpec((SUBLANES, d), lambda l, j: (0, 0)),
                  pl.BlockSpec((1, d, tn), lambda l, j: (l, 0, j)),
                  pl.BlockSpec((1, 1, tn), lambda l, j: (l, 0, j))],
        out_specs=pl.BlockSpec((1, SUBLANES, tn), lambda l, j: (l, 0, j)),
        out_shape=jax.ShapeDtypeStruct((depth, SUBLANES, n), F32),
        compiler_params=_params("parallel", "parallel"),
        name="ada_mod",
    )(cs, ada_w, ada_b.reshape(depth, 1, n))


def _modulated_norm(x, gain, shift, scale):
    ms = jnp.mean(x * x, axis=-1, keepdims=True)
    return (x * lax.rsqrt(ms + EPS)) * gain * (1.0 + scale) + shift


def _proj_kernel(*refs, nq, nk, rope):
    if rope:
        (x_ref, mod_ref, g_ref, w_ref, gq_ref, gk_ref, bd_ref, cos_ref, sa_ref, sb_ref,
         q_ref, kt_ref, v_ref) = refs
    else:
        x_ref, mod_ref, g_ref, w_ref, gq_ref, gk_ref, bd_ref, q_ref, kt_ref, v_ref = refs
    h = _modulated_norm(x_ref[0], g_ref[...], mod_ref[0, 0:1, :], mod_ref[0, 1:2, :])
    p = jnp.dot(h.astype(BF16), w_ref[...], preferred_element_type=F32)
    bd = bd_ref[...]

    def head_norm(c0, gain):
        yc = p[:, c0:c0 + LANES]
        ss = jnp.dot((yc * yc).astype(BF16), bd, preferred_element_type=F32)
        yn = yc * lax.rsqrt(ss + EPS) * gain
        if rope:
            yn = (yn * cos_ref[...] + pltpu.roll(yn, LANES - 16, 1) * sa_ref[...]
                  + pltpu.roll(yn, 16, 1) * sb_ref[...])
        return yn

    for c in range(nq // LANES):
        c0 = c * LANES
        q_ref[0, :, c0:c0 + LANES] = head_norm(c0, gq_ref[:, c0:c0 + LANES]).astype(BF16)
    for c in range(nk // LANES):
        c0 = c * LANES
        yn = head_norm(nq + c0, gk_ref[:, c0:c0 + LANES])
        kt_ref[0, c0:c0 + LANES, :] = yn.T.astype(BF16)
    v_ref[0] = p[:, nq + nk:].astype(BF16)


def _proj(x, mod, gain, w, gq, gk, bd, rope_tabs, nq, nk, nv):
    b, l, d = x.shape
    tm = ROW_TILE
    n = nq + nk + nv
    mod_map = (lambda i, j: (i, 0, 0)) if mod.shape[0] > 1 else (lambda i, j: (0, 0, 0))
    in_specs = [pl.BlockSpec((1, tm, d), lambda i, j: (i, j, 0)),
                pl.BlockSpec((1, 6, d), mod_map),
                pl.BlockSpec((1, d), lambda i, j: (0, 0)),
                pl.BlockSpec((d, n), lambda i, j: (0, 0)),
                pl.BlockSpec((1, nq), lambda i, j: (0, 0)),
                pl.BlockSpec((1, nk), lambda i, j: (0, 0)),
                pl.BlockSpec((LANES, LANES), lambda i, j: (0, 0))]
    args = [x, mod, gain, w, gq, gk, bd]
    if rope_tabs is not None:
        in_specs += [pl.BlockSpec((tm, LANES), lambda i, j: (j, 0))] * 3
        args += list(rope_tabs)
    return pl.pallas_call(
        functools.partial(_proj_kernel, nq=nq, nk=nk, rope=rope_tabs is not None),
        grid=(b, l // tm),
        in_specs=in_specs,
        out_specs=[pl.BlockSpec((1, tm, nq), lambda i, j: (i, j, 0)),
                   pl.BlockSpec((1, nk, tm), lambda i, j: (i, 0, j)),
                   pl.BlockSpec((1, tm, nv), lambda i, j: (i, j, 0))],
        out_shape=[jax.ShapeDtypeStruct((b, l, nq), BF16),
                   jax.ShapeDtypeStruct((b, nk, l), BF16),
                   jax.ShapeDtypeStruct((b, l, nv), BF16)],
        compiler_params=_params("parallel", "parallel"),
        name="in_proj",
    )(*args)


def _online_update(g, s, v, m_ref, l_ref, acc_ref):
    m_prev = m_ref[g]
    m_new = jnp.maximum(m_prev, jnp.max(s, axis=-1, keepdims=True))
    alpha = jnp.exp2(m_prev - m_new)
    p = jnp.exp2(s - m_new)
    l_ref[g] = alpha * l_ref[g] + jnp.sum(p, axis=-1, keepdims=True)
    acc_ref[g] = alpha * acc_ref[g] + jnp.dot(p.astype(BF16), v, preferred_element_type=F32)
    m_ref[g] = m_new


def _attn_sweep(kt_rows, v_lane, qs_ref, ktl_ref, vl_ref, ktc_ref, vc_ref, s_ref, m_ref, l_ref, acc_ref):
    tq = qs_ref.shape[1] // len(kt_rows[0])
    m_ref[...] = jnp.full(m_ref.shape, -jnp.inf, F32)
    l_ref[...] = jnp.zeros(l_ref.shape, F32)
    acc_ref[...] = jnp.zeros(acc_ref.shape, F32)

    def scores(g, kt):
        rows = kt_rows[g]
        if len(set(rows)) == 1:
            return jnp.dot(qs_ref[g], kt(rows[0]), preferred_element_type=F32)
        return jnp.concatenate(
            [jnp.dot(qs_ref[g, i * tq:(i + 1) * tq, :], kt(r), preferred_element_type=F32)
             for i, r in enumerate(rows)], axis=0)

    if ktl_ref is not None:
        s_len = ktl_ref.shape[2]
        tk = s_ref.shape[2]
        n_chunks = s_len // tk

        def kt_chunk(c):
            off = pl.multiple_of(c * tk, tk)
            return lambda r: ktl_ref[0, r:r + HEAD_DIM, pl.ds(off, tk)]

        def v_chunk(c, g):
            off = pl.multiple_of(c * tk, tk)
            return vl_ref[0, pl.ds(off, tk), v_lane[g]:v_lane[g] + LANES]

        s_ref[0] = scores(0, kt_chunk(0))

        def body(c, carry):
            s_ref[1] = scores(1, kt_chunk(c))
            _online_update(0, s_ref[0], v_chunk(c, 0), m_ref, l_ref, acc_ref)
            s_ref[0] = scores(0, kt_chunk(jnp.minimum(c + 1, n_chunks - 1)))
            _online_update(1, s_ref[1], v_chunk(c, 1), m_ref, l_ref, acc_ref)
            return carry

        lax.fori_loop(0, n_chunks, body, 0)
    for g in range(2):
        s = scores(g, lambda r: ktc_ref[0, r:r + HEAD_DIM, :])
        _online_update(g, s, vc_ref[0, :, v_lane[g]:v_lane[g] + LANES], m_ref, l_ref, acc_ref)


def _split_refs(refs, has_lat, n_extra):
    refs = list(refs)
    q_ref = refs.pop(0)
    ktl_ref, vl_ref = (refs.pop(0), refs.pop(0)) if has_lat else (None, None)
    ktc_ref, vc_ref = refs.pop(0), refs.pop(0)
    extras = [refs.pop(0) for _ in range(n_extra)]
    o_ref, qs_ref, m_ref, l_ref, acc_ref = refs[:5]
    s_ref = refs[5] if has_lat else None
    return q_ref, ktl_ref, vl_ref, ktc_ref, vc_ref, extras, o_ref, qs_ref, s_ref, m_ref, l_ref, acc_ref


def _gqa_kernel(*refs, has_lat):
    (q_ref, ktl_ref, vl_ref, ktc_ref, vc_ref, _, o_ref, qs_ref, s_ref, m_ref, l_ref,
     acc_ref) = _split_refs(refs, has_lat, 0)
    tq = q_ref.shape[1]
    group = A_HEADS // A_KV_HEADS
    for h in range(A_HEADS):
        qs_ref[h // group, (h % group) * tq:(h % group + 1) * tq, :] = (
            q_ref[0, :, h * HEAD_DIM:(h + 1) * HEAD_DIM])
    _attn_sweep([[g * HEAD_DIM] * group for g in range(A_KV_HEADS)], [0, 0],
                qs_ref, ktl_ref, vl_ref, ktc_ref, vc_ref, s_ref, m_ref, l_ref, acc_ref)
    lane = lax.broadcasted_iota(jnp.int32, (tq, LANES), 1)
    for j in range(A_HEADS // 2):
        g, i = (2 * j) // group, (2 * j) % group
        rows_a = slice(i * tq, (i + 1) * tq)
        rows_c = slice((i + 1) * tq, (i + 2) * tq)
        a = acc_ref[g, rows_a, :] / l_ref[g, rows_a, :]
        c = acc_ref[g, rows_c, :] / l_ref[g, rows_c, :]
        if g == 0:
            c = pltpu.roll(c, HEAD_DIM, 1)
        else:
            a = pltpu.roll(a, HEAD_DIM, 1)
        o_ref[0, :, j * LANES:(j + 1) * LANES] = jnp.where(lane < HEAD_DIM, a, c).astype(BF16)


def _diff_kernel(*refs, has_lat, lam_init):
    (q_ref, ktl_ref, vl_ref, ktc_ref, vc_ref, (lam_ref, sub_ref), o_ref, qs_ref, s_ref, m_ref, l_ref,
     acc_ref) = _split_refs(refs, has_lat, 2)
    tq = q_ref.shape[1]
    for g in range(2):
        for i in range(2):
            c0 = g * LANES + i * HEAD_DIM
            qs_ref[g, i * tq:(i + 1) * tq, :] = q_ref[0, :, c0:c0 + HEAD_DIM]
    _attn_sweep([[g * LANES, g * LANES + HEAD_DIM] for g in range(2)], [0, LANES],
                qs_ref, ktl_ref, vl_ref, ktc_ref, vc_ref, s_ref, m_ref, l_ref, acc_ref)
    lf = lam_ref[...]
    lam = (jnp.exp(jnp.sum(lf[0:1] * lf[1:2], axis=-1, keepdims=True))
           - jnp.exp(jnp.sum(lf[2:3] * lf[3:4], axis=-1, keepdims=True)) + lam_init)
    for g in range(2):
        o = (acc_ref[g, 0:tq, :] / l_ref[g, 0:tq, :]
             - lam * (acc_ref[g, tq:2 * tq, :] / l_ref[g, tq:2 * tq, :]))
        ms = jnp.mean(o * o, axis=-1, keepdims=True)
        o = (o * lax.rsqrt(ms + EPS)) * sub_ref[...] * (1.0 - lam_init)
        o_ref[0, :, g * LANES:(g + 1) * LANES] = o.astype(BF16)


def _attn_scratch(rows, tk):
    scratch = [pltpu.VMEM((2, rows, HEAD_DIM), BF16), pltpu.VMEM((2, rows, 1), F32),
               pltpu.VMEM((2, rows, 1), F32), pltpu.VMEM((2, rows, LANES), F32)]
    if tk:
        scratch.append(pltpu.VMEM((2, rows, tk), F32))
    return scratch


_KV_A_BLOCK = B_HEADS * 2 * HEAD_DIM // LANES


def _gqa(q, kt_lat, v_lat, kt_ctx, v_ctx):
    b, lq, _ = q.shape
    tq = min(Q_TILE, lq)
    nc = kt_ctx.shape[2]
    has_lat = kt_lat is not None
    blk = _KV_A_BLOCK
    in_specs = [pl.BlockSpec((1, tq, A_HEADS * HEAD_DIM), lambda i, j: (i, j, 0))]
    args = [q]
    tk = 0
    if has_lat:
        s = kt_lat.shape[2]
        tk = min(K_CHUNK, s)
        in_specs += [pl.BlockSpec((1, LANES, s), lambda i, j: (i, blk, 0)),
                     pl.BlockSpec((1, s, LANES), lambda i, j: (i, 0, blk))]
        args += [kt_lat, v_lat]
    in_specs += [pl.BlockSpec((1, LANES, nc), lambda i, j: (i, blk, 0)),
                 pl.BlockSpec((1, nc, LANES), lambda i, j: (i, 0, blk))]
    args += [kt_ctx, v_ctx]
    return pl.pallas_call(
        functools.partial(_gqa_kernel, has_lat=has_lat),
        grid=(b, lq // tq),
        in_specs=in_specs,
        out_specs=pl.BlockSpec((1, tq, A_HEADS * HEAD_DIM), lambda i, j: (i, j, 0)),
        out_shape=jax.ShapeDtypeStruct((b, lq, A_HEADS * HEAD_DIM), BF16),
        scratch_shapes=_attn_scratch(A_HEADS // A_KV_HEADS * tq, tk),
        compiler_params=_params("parallel", "parallel"),
        name="gqa_attn",
    )(*args)


def _diff(q, kt_lat, v_lat, kt_ctx, v_ctx, b_lambda, b_subln, lam_init):
    b, lq, _ = q.shape
    tq = min(Q_TILE, lq)
    nc = kt_ctx.shape[2]
    has_lat = kt_lat is not None
    w = 2 * LANES
    q0 = A_HEADS * HEAD_DIM // w
    in_specs = [pl.BlockSpec((1, tq, w), lambda i, h, j: (i, j, q0 + h))]
    args = [q]
    tk = 0
    if has_lat:
        s = kt_lat.shape[2]
        tk = min(K_CHUNK, s)
        in_specs += [pl.BlockSpec((1, w, s), lambda i, h, j: (i, h, 0)),
                     pl.BlockSpec((1, s, w), lambda i, h, j: (i, 0, h))]
        args += [kt_lat, v_lat]
    in_specs += [pl.BlockSpec((1, w, nc), lambda i, h, j: (i, h, 0)),
                 pl.BlockSpec((1, nc, w), lambda i, h, j: (i, 0, h)),
                 pl.BlockSpec((4, HEAD_DIM), lambda i, h, j: (0, 0)),
                 pl.BlockSpec((1, LANES), lambda i, h, j: (0, 0))]
    args += [kt_ctx, v_ctx, b_lambda, b_subln]
    return pl.pallas_call(
        functools.partial(_diff_kernel, has_lat=has_lat, lam_init=lam_init),
        grid=(b, B_HEADS // 2, lq // tq),
        in_specs=in_specs,
        out_specs=pl.BlockSpec((1, tq, w), lambda i, h, j: (i, j, h)),
        out_shape=jax.ShapeDtypeStruct((b, lq, B_HEADS * LANES), BF16),
        scratch_shapes=_attn_scratch(2 * tq, tk),
        compiler_params=_params("parallel", "parallel", "parallel"),
        name="diff_attn",
    )(*args)


def _na_kernel(q_ref, km_ref, k0_ref, kp_ref, kc_ref, vm_ref, v0_ref, vp_ref, vc_ref, bias_ref,
               o_ref, *, rows):
    i = pl.program_id(1)
    tq = q_ref.shape[1]
    nb = 3 * tq
    qrow = NA_BLOCK_ROWS * i + lax.broadcasted_iota(jnp.int32, (tq, nb), 0) // GRID_W
    krow = NA_BLOCK_ROWS * (i - 1) + lax.broadcasted_iota(jnp.int32, (tq, nb), 1) // GRID_W
    rs = jnp.clip(qrow - NA_ROWS // 2, 0, rows - NA_ROWS)
    row_mask = jnp.where((krow >= rs) & (krow < rs + NA_ROWS), 0.0, NEG).astype(F32)
    lane = lax.broadcasted_iota(jnp.int32, (tq, LANES), 1)
    kts = (km_ref, k0_ref, kp_ref)
    vs = (vm_ref, v0_ref, vp_ref)
    for pair in range(C_HEADS // 2):
        outs = []
        for h in (2 * pair, 2 * pair + 1):
            r0 = h * HEAD_DIM
            q = q_ref[0, :, r0:r0 + HEAD_DIM]
            s_band = jnp.concatenate(
                [jnp.dot(q, k[0, r0:r0 + HEAD_DIM, :], preferred_element_type=F32) for k in kts],
                axis=1) + bias_ref[h] + row_mask
            s_ctx = jnp.dot(q, kc_ref[0, r0:r0 + HEAD_DIM, :], preferred_element_type=F32)
            m = jnp.maximum(jnp.max(s_band, axis=-1, keepdims=True),
                            jnp.max(s_ctx, axis=-1, keepdims=True))
            p_band = jnp.exp2(s_band - m)
            p_ctx = jnp.exp2(s_ctx - m)
            denom = jnp.sum(p_band, axis=-1, keepdims=True) + jnp.sum(p_ctx, axis=-1, keepdims=True)
            c0 = pair * LANES
            pv = jnp.dot(p_ctx.astype(BF16), vc_ref[0, :, c0:c0 + LANES], preferred_element_type=F32)
            for t, v in enumerate(vs):
                pv += jnp.dot(p_band[:, t * tq:(t + 1) * tq].astype(BF16), v[0, :, c0:c0 + LANES],
                              preferred_element_type=F32)
            outs.append(pv / denom)
        a, c = outs
        o_ref[0, :, pair * LANES:(pair + 1) * LANES] = jnp.where(lane < HEAD_DIM, a, c).astype(BF16)


def _na(q, kt_lat, v_lat, kt_ctx, v_ctx, bias):
    b, s, w = q.shape
    tq = NA_BLOCK_ROWS * GRID_W
    nblk = s // tq
    nc = kt_ctx.shape[2]
    rows = s // GRID_W

    def kt_spec(d):
        return pl.BlockSpec((1, w, tq), lambda i, j: (i, 0, jnp.clip(j + d, 0, nblk - 1)))

    def v_spec(d):
        return pl.BlockSpec((1, tq, w), lambda i, j: (i, jnp.clip(j + d, 0, nblk - 1), 0))

    return pl.pallas_call(
        functools.partial(_na_kernel, rows=rows),
        grid=(b, nblk),
        in_specs=[pl.BlockSpec((1, tq, w), lambda i, j: (i, j, 0)),
                  kt_spec(-1), kt_spec(0), kt_spec(1),
                  pl.BlockSpec((1, w, nc), lambda i, j: (i, 0, 0)),
                  v_spec(-1), v_spec(0), v_spec(1),
                  pl.BlockSpec((1, nc, w), lambda i, j: (i, 0, 0)),
                  pl.BlockSpec(bias.shape, lambda i, j: (0, 0, 0))],
        out_specs=pl.BlockSpec((1, tq, w), lambda i, j: (i, j, 0)),
        out_shape=jax.ShapeDtypeStruct((b, s, w), BF16),
        compiler_params=_params("parallel", "parallel"),
        name="na_attn",
    )(q, kt_lat, kt_lat, kt_lat, kt_ctx, v_lat, v_lat, v_lat, v_ctx, bias)


def _na_bias_table(rpb):
    heads = rpb.shape[0]
    nr, nc = 2 * NA_ROWS - 1, 2 * NA_COLS - 1
    j = np.arange(GRID_W)
    dc = j[None, :] - j[:, None] + (NA_COLS - 1)
    cs = np.clip(j - NA_COLS // 2, 0, GRID_W - NA_COLS)
    col_ok = (j[None, :] >= cs[:, None]) & (j[None, :] < cs[:, None] + NA_COLS)
    onehot = (np.arange(nc)[:, None, None] == dc[None]) & col_ok[None]
    cols = jnp.einsum('hrm,mjk->hrjk', rpb, jnp.asarray(onehot, F32),
                      precision=lax.Precision.HIGHEST)
    cols = jnp.where(col_ok[None, None], cols, NEG)
    band = 3 * NA_BLOCK_ROWS
    per_q = [cols[:, NA_ROWS - 1 - NA_BLOCK_ROWS - qr:NA_ROWS - 1 - NA_BLOCK_ROWS - qr + band]
             for qr in range(NA_BLOCK_ROWS)]
    t = jnp.stack(per_q, axis=1)
    assert t.shape == (heads, NA_BLOCK_ROWS, band, GRID_W, GRID_W) and nr == band + NA_BLOCK_ROWS - 1
    return t.transpose(0, 1, 3, 2, 4).reshape(heads, NA_BLOCK_ROWS * GRID_W, band * GRID_W)


def _out_kernel(o1_ref, o2_ref, w_ref, x_ref, mod_ref, g_ref, x1_ref, ht_ref):
    half = o1_ref.shape[2]
    o = (jnp.dot(o1_ref[0], w_ref[0:half, :], preferred_element_type=F32)
         + jnp.dot(o2_ref[0], w_ref[half:, :], preferred_element_type=F32))
    x1 = x_ref[0] + mod_ref[0, 2:3, :] * o
    x1_ref[0] = x1
    h2 = _modulated_norm(x1, g_ref[...], mod_ref[0, 3:4, :], mod_ref[0, 4:5, :])
    ht_ref[0] = h2.T.astype(BF16)


def _out(o1, o1_blk, o2, o2_blk, w, x, mod, gain):
    b, l, d = x.shape
    tm = min(ROW_TILE, l)
    half = d // 2
    mod_map = (lambda i, j: (i, 0, 0)) if mod.shape[0] > 1 else (lambda i, j: (0, 0, 0))
    return pl.pallas_call(
        _out_kernel,
        grid=(b, l // tm),
        in_specs=[pl.BlockSpec((1, tm, half), lambda i, j: (i, j, o1_blk)),
                  pl.BlockSpec((1, tm, half), lambda i, j: (i, j, o2_blk)),
                  pl.BlockSpec((d, d), lambda i, j: (0, 0)),
                  pl.BlockSpec((1, tm, d), lambda i, j: (i, j, 0)),
                  pl.BlockSpec((1, 6, d), mod_map),
                  pl.BlockSpec((1, d), lambda i, j: (0, 0))],
        out_specs=[pl.BlockSpec((1, tm, d), lambda i, j: (i, j, 0)),
                   pl.BlockSpec((1, d, tm), lambda i, j: (i, 0, j))],
        out_shape=[jax.ShapeDtypeStruct((b, l, d), F32),
                   jax.ShapeDtypeStruct((b, d, l), BF16)],
        compiler_params=_params("parallel", "parallel"),
        name="out_proj",
    )(o1, o2, w, x, mod, gain)


def _top16(s, iota, exact):
    rank = jnp.full(s.shape, float(PEER_TOPK), F32)
    big = float(s.shape[0])
    vals = []
    for r in range(PEER_TOPK):
        m = jnp.max(s, axis=0, keepdims=True)
        hit = s == m
        if exact:
            hit = iota == jnp.min(jnp.where(hit, iota, big), axis=0, keepdims=True)
        rank = jnp.where(hit, float(r), rank)
        s = jnp.where(hit, -jnp.inf, s)
        vals.append(m)
    return vals, rank


def _rows_tile(rows, iota8):
    t = jnp.zeros((SUBLANES,) + rows[0].shape[1:], F32)
    for k, r in enumerate(rows):
        t = jnp.where(iota8 == k, r, t)
    return t


_CAND_COUNTS = [PEER_TOPK // (a + 1) for a in range(SUBLANES)]


def _retrieve(s1, s2, exact):
    iota = lax.broadcasted_iota(jnp.int32, (PEER_N_KEYS, LANES), 0).astype(F32)
    iota8 = lax.broadcasted_iota(jnp.int32, (SUBLANES, LANES), 0)
    iota_c = lax.broadcasted_iota(jnp.int32, (SUBLANES * (SUBLANES + 2), LANES), 0).astype(F32)
    v1, r1 = _top16(s1, iota, exact)
    v2, r2 = _top16(s2, iota, exact)
    e1 = [jnp.exp(v - v1[0]) for v in v1]
    e2 = [jnp.exp(v - v2[0]) for v in v2]
    v2_lo, v2_hi = _rows_tile(v2[:8], iota8), _rows_tile(v2[8:], iota8)
    e2_lo, e2_hi = _rows_tile(e2[:8], iota8), _rows_tile(e2[8:], iota8)
    v1_hi, e1_hi = _rows_tile(v1[8:], iota8), _rows_tile(e1[8:], iota8)
    cand, ecand = [], []
    for a, cnt in enumerate(_CAND_COUNTS):
        cand.append(jnp.where(iota8 < cnt, v1[a] + v2_lo, -jnp.inf))
        ecand.append(e1[a] * e2_lo)
        if a == 0:
            cand.append(v1[0] + v2_hi)
            ecand.append(e1[0] * e2_hi)
    cand.append(v1_hi + v2[0])
    ecand.append(e1_hi * e2[0])
    cand = jnp.concatenate(cand, axis=0)
    ecand = jnp.concatenate(ecand, axis=0)
    _, rc = _top16(cand, iota_c, exact)
    sel = (rc < float(PEER_TOPK)).astype(F32)
    ranked = (jnp.sum(sel, axis=0, keepdims=True)
              + jnp.sum((r1 < float(PEER_TOPK)).astype(F32), axis=0, keepdims=True)
              + jnp.sum((r2 < float(PEER_TOPK)).astype(F32), axis=0, keepdims=True))
    z = jnp.sum(sel * ecand, axis=0, keepdims=True)
    n_rows = [jnp.sum(sel[0:2 * SUBLANES], axis=0, keepdims=True)]
    for a in range(1, SUBLANES):
        n_rows.append(jnp.sum(sel[SUBLANES * (a + 1):SUBLANES * (a + 2)], axis=0, keepdims=True))
    n_hi = sel[SUBLANES * (SUBLANES + 1):]
    n1 = jnp.zeros(s1.shape, F32)
    for a in range(PEER_TOPK):
        na = n_rows[a] if a < SUBLANES else n_hi[a - SUBLANES:a - SUBLANES + 1]
        n1 = jnp.where(r1 == float(a), na, n1)
    return n1, jnp.exp(s1 - v1[0]) / z, r2, jnp.exp(s2 - v2[0]), ranked


def _topk_kernel(ht_ref, wq_ref, keys_ref, n1_ref, c1_ref, b2_ref, e2_ref, qt_ref, s_ref):
    chunks = ht_ref.shape[2] // LANES
    qt_ref[...] = jnp.dot(wq_ref[...], ht_ref[0], preferred_element_type=F32).astype(BF16)
    for hp in range(2 * PEER_HEADS):
        s_ref[hp] = jnp.dot(keys_ref[hp], qt_ref[hp * PEER_N_KEYS:(hp + 1) * PEER_N_KEYS, :],
                            preferred_element_type=F32)

    def unit(i, carry):
        h = i // chunks
        sl = pl.ds(pl.multiple_of((i % chunks) * LANES, LANES), LANES)
        s1 = s_ref[2 * h, :, sl]
        s2 = s_ref[2 * h + 1, :, sl]

        def emit(exact):
            n1, c1, b2, e2, ranked = _retrieve(s1, s2, exact)
            n1_ref[0, h, :, sl] = n1
            c1_ref[0, h, :, sl] = c1
            b2_ref[0, h, :, sl] = pltpu.bitcast(b2.astype(BF16), jnp.int32)
            e2_ref[0, h, :, sl] = pltpu.bitcast(e2.astype(BF16), jnp.int32)
            return ranked

        ranked = emit(False)

        @pl.when(jnp.max(ranked) > float(3 * PEER_TOPK))
        def _():
            emit(True)

        return carry

    lax.fori_loop(0, PEER_HEADS * chunks, unit, 0)


def _peer_topk(ht, wq_t, keys, tt):
    b, d, l = ht.shape
    nqd = wq_t.shape[0]
    shp = (b, PEER_HEADS, PEER_N_KEYS, l)
    shp_packed = (b, PEER_HEADS, PEER_N_KEYS // 2, l)
    ospec = pl.BlockSpec((1, PEER_HEADS, PEER_N_KEYS, tt), lambda i, j: (i, 0, 0, j))
    ospec_packed = pl.BlockSpec((1, PEER_HEADS, PEER_N_KEYS // 2, tt), lambda i, j: (i, 0, 0, j))
    return pl.pallas_call(
        _topk_kernel,
        grid=(b, l // tt),
        in_specs=[pl.BlockSpec((1, d, tt), lambda i, j: (i, 0, j)),
                  pl.BlockSpec((nqd, d), lambda i, j: (0, 0)),
                  pl.BlockSpec(keys.shape, lambda i, j: (0, 0, 0))],
        out_specs=[ospec, ospec, ospec_packed, ospec_packed],
        out_shape=[jax.ShapeDtypeStruct(shp, F32), jax.ShapeDtypeStruct(shp, F32),
                   jax.ShapeDtypeStruct(shp_packed, jnp.int32),
                   jax.ShapeDtypeStruct(shp_packed, jnp.int32)],
        scratch_shapes=[pltpu.VMEM((nqd, tt), BF16),
                        pltpu.VMEM((2 * PEER_HEADS, PEER_N_KEYS, tt), F32)],
        compiler_params=_params("parallel", "parallel"),
        name="peer_topk",
    )(ht, wq_t, keys)


def _peer_kernel(ht_ref, n1_ref, c1_ref, b2_ref, e2_ref, u_ref, vt_ref, x_ref, mod_ref, o_ref,
                 a_ref, g_ref, acc_ref):
    e = pl.program_id(2)
    et = u_ref.shape[0]

    @pl.when(e == 0)
    def _():
        acc_ref[...] = jnp.zeros(acc_ref.shape, F32)

    a_ref[...] = jnp.dot(u_ref[...], ht_ref[0], preferred_element_type=F32)
    tile3 = (PEER_N_KEYS // BF16_ROWS, BF16_ROWS, LANES)
    for k in range(et // PEER_N_KEYS):
        rows = slice(k * PEER_N_KEYS, (k + 1) * PEER_N_KEYS)
        for c in range(ht_ref.shape[2] // LANES):
            sl = slice(c * LANES, (c + 1) * LANES)
            w = jnp.zeros(tile3, BF16)
            for h in range(PEER_HEADS):
                n_t = jnp.broadcast_to(n1_ref[0, h, k:k + 1, sl], tile3[1:]).astype(BF16)[None]
                c_t = jnp.broadcast_to(c1_ref[0, h, k:k + 1, sl], tile3[1:]).astype(BF16)[None]
                b2 = pltpu.bitcast(b2_ref[0, h, :, sl], BF16).reshape(tile3)
                e2 = pltpu.bitcast(e2_ref[0, h, :, sl], BF16).reshape(tile3)
                w = w + jnp.clip(n_t - b2, 0, 1) * (e2 * c_t)
            a = a_ref[rows, sl]
            z = a * (a * a * (-2.0 * _GELU_K * 0.044715 * _LOG2E) - 2.0 * _GELU_K * _LOG2E)
            act = (a / (1.0 + jnp.exp2(z))).astype(BF16)
            g_ref[rows, sl] = (w * act.reshape(tile3)).reshape(PEER_N_KEYS, LANES)
    acc_ref[...] += jnp.dot(vt_ref[...], g_ref[...], preferred_element_type=F32)

    @pl.when(e == pl.num_programs(2) - 1)
    def _():
        o_ref[0] = x_ref[0] + mod_ref[0, 5:6, :] * acc_ref[...].T


def _peer_dense(ht, n1, c1, b2, e2, u, vt, x, mod, tt):
    b, d, l = ht.shape
    ne = u.shape[0]
    et = PEER_EXPERTS
    mod_map = (lambda i, j, k: (i, 0, 0)) if mod.shape[0] > 1 else (lambda i, j, k: (0, 0, 0))
    sel_spec = pl.BlockSpec((1, PEER_HEADS, PEER_N_KEYS // 2, tt), lambda i, j, k: (i, 0, 0, j))
    row_spec = pl.BlockSpec((1, PEER_HEADS, et // PEER_N_KEYS, tt), lambda i, j, k: (i, 0, k, j))
    return pl.pallas_call(
        _peer_kernel,
        grid=(b, l // tt, ne // et),
        in_specs=[pl.BlockSpec((1, d, tt), lambda i, j, k: (i, 0, j)),
                  row_spec, row_spec, sel_spec, sel_spec,
                  pl.BlockSpec((et, d), lambda i, j, k: (k, 0)),
                  pl.BlockSpec((d, et), lambda i, j, k: (0, k)),
                  pl.BlockSpec((1, tt, d), lambda i, j, k: (i, j, 0)),
                  pl.BlockSpec((1, 6, d), mod_map)],
        out_specs=pl.BlockSpec((1, tt, d), lambda i, j, k: (i, j, 0)),
        out_shape=jax.ShapeDtypeStruct((b, l, d), F32),
        scratch_shapes=[pltpu.VMEM((et, tt), F32), pltpu.VMEM((et, tt), BF16),
                        pltpu.VMEM((d, tt), F32)],
        compiler_params=_params("parallel", "parallel", "arbitrary"),
        name="peer_mix",
    )(ht, n1, c1, b2, e2, u, vt, x, mod)


def _peer(ht, x1, mod, wq_t, keys, u, vt):
    tt = min(PEER_TOKENS, ht.shape[2])
    n1, c1, b2, e2 = _peer_topk(ht, wq_t, keys, tt)
    return _peer_dense(ht, n1, c1, b2, e2, u, vt, x1, mod, tt)


def _rope_tables(s):
    t = np.arange(s)
    pos = np.stack([t // GRID_W, t % GRID_W], axis=1).astype(np.float32)
    axis_dim = HEAD_DIM // 2
    freqs = 1.0 / (ROPE_THETA ** (jnp.arange(0, axis_dim, 2, dtype=F32) / axis_dim))
    ang = jnp.asarray(pos)[:, :, None] * freqs[None, None, :]
    cos, sin = jnp.cos(ang), jnp.sin(ang)
    zero = jnp.zeros_like(sin)
    def lanes(first_half, second_half):
        p = jnp.stack([first_half, second_half], axis=2).reshape(s, HEAD_DIM)
        return jnp.concatenate([p, p], axis=1)
    return lanes(cos, cos), lanes(-sin, zero), lanes(zero, sin)


def _block_diag_mean():
    i = np.arange(LANES)
    return jnp.asarray((i[:, None] // HEAD_DIM == i[None, :] // HEAD_DIM) / HEAD_DIM, BF16)


def kernel(x, c, ctx, c_ctx, ada_w, ada_b, norm_g, ab_w_in, ab_w_out, a_q_norm, a_k_norm, b_q_norm,
           b_k_norm, b_lambda, b_subln, c_w_in, c_w_out, c_q_norm, c_k_norm, c_rpb, peer_w_q,
           peer_keys, peer_u, peer_v):
    bsz, s, d = x.shape
    depth = ada_w.shape[0]
    assert depth == 2, "one attention-pair layer followed by one (final) neighbourhood layer"
    log2e = math.log2(math.e)
    scale = HEAD_DIM ** -0.5 * log2e

    cs = jnp.zeros((SUBLANES, d), F32).at[:bsz].set(c).at[bsz].set(c_ctx)
    mods = _ada(cs, ada_w, ada_b)
    bd = _block_diag_mean()
    rope = _rope_tables(s)

    for l in range(depth):
        last = l == depth - 1
        mod_lat = mods[l, :bsz].reshape(bsz, 6, d)
        mod_ctx = mods[l, bsz:bsz + 1].reshape(1, 6, d)
        g1 = norm_g[l, 0].reshape(1, d)
        g2 = norm_g[l, 1].reshape(1, d)
        i = l // 2
        wq_t = peer_w_q[l].T.astype(BF16)
        keys = peer_keys[l].reshape(2 * PEER_HEADS, PEER_N_KEYS, -1).astype(BF16)
        u = peer_u[l].astype(BF16)
        vt = peer_v[l].T.astype(BF16)
        if l % 2 == 0:
            lam_init = 0.8 - 0.6 * math.exp(-0.3 * l)
            qa, ka, va, qb, kb, vb = jnp.split(
                ab_w_in[i], np.cumsum([512, 128, 128, 512, 512, 512])[:-1].tolist(), axis=1)
            w = jnp.concatenate([qa, qb, kb, ka, vb, va], axis=1).astype(BF16)
            nq, nk, nv = 1024, 640, 640
            gq = jnp.concatenate([jnp.tile(a_q_norm[i], A_HEADS), jnp.tile(b_q_norm[i], 2 * B_HEADS)])
            gk = jnp.concatenate([jnp.tile(b_k_norm[i], 2 * B_HEADS), jnp.tile(a_k_norm[i], A_KV_HEADS)])
            gq = (gq * scale).reshape(1, nq)
            gk = gk.reshape(1, nk)
            q_l, kt_l, v_l = _proj(x, mod_lat, g1, w, gq, gk, bd, rope, nq, nk, nv)
            q_c, kt_c, v_c = _proj(ctx, mod_ctx, g1, w, gq, gk, bd, None, nq, nk, nv)
            lam2 = b_lambda[i]
            sub = b_subln[i].reshape(1, LANES)
            oa = _gqa(q_l, kt_l, v_l, kt_c, v_c)
            ob = _diff(q_l, kt_l, v_l, kt_c, v_c, lam2, sub, lam_init)
            w_out = ab_w_out[i].astype(BF16)
            x1, ht = _out(oa, 0, ob, 0, w_out, x, mod_lat, g2)
            if not last:
                oa_c = _gqa(q_c, None, None, kt_c, v_c)
                ob_c = _diff(q_c, None, None, kt_c, v_c, lam2, sub, lam_init)
                ctx1, ht_c = _out(oa_c, 0, ob_c, 0, w_out, ctx, mod_ctx, g2)
        else:
            w = c_w_in[i].astype(BF16)
            nq = nk = nv = C_HEADS * HEAD_DIM
            gq = (jnp.tile(c_q_norm[i], C_HEADS) * scale).reshape(1, nq)
            gk = jnp.tile(c_k_norm[i], C_HEADS).reshape(1, nk)
            q_l, kt_l, v_l = _proj(x, mod_lat, g1, w, gq, gk, bd, None, nq, nk, nv)
            q_c, kt_c, v_c = _proj(ctx, mod_ctx, g1, w, gq, gk, bd, None, nq, nk, nv)
            o = _na(q_l, kt_l, v_l, kt_c, v_c, _na_bias_table(c_rpb[i] * log2e))
            w_out = c_w_out[i].astype(BF16)
            x1, ht = _out(o, 0, o, 1, w_out, x, mod_lat, g2)
        x = _peer(ht, x1, mod_lat, wq_t, keys, u, vt)
        if not last:
            ctx = _peer(ht_c, ctx1, mod_ctx, wq_t, keys, u, vt)
    return x
```

```python
import functools
import math

import numpy as np
import jax
import jax.numpy as jnp
from jax import lax
from jax.experimental import pallas as pl
from jax.experimental.pallas import tpu as pltpu

F32 = jnp.float32
BF16 = jnp.bfloat16

D_MODEL = 1024
GRID_W = 64
HEAD_DIM = 64
ROPE_THETA = 10000.0
EPS = 1e-6
A_HEADS = 8
A_KV_HEADS = 2
B_HEADS = 4
C_HEADS = 16
NA_ROWS = 8
NA_COLS = 16
PEER_HEADS = 8
PEER_N_KEYS = 128
PEER_TOPK = 16

LANES = 128
SUBLANES = 8
BF16_ROWS = 16
VMEM_LIMIT = 56 * 1024 * 1024
NEG = -1e30

ROW_TILE = 512
Q_TILE = 256
SCORE_ELEMS = 1536 * 1024
NA_BLOCK_ROWS = 4
PEER_TOKENS = 512
PEER_EXPERTS = 1024
_GELU_K = math.sqrt(2.0 / math.pi)
_LOG2E = math.log2(math.e)


def _params(*sem):
    return pltpu.CompilerParams(dimension_semantics=sem, vmem_limit_bytes=VMEM_LIMIT)


def _ada_kernel(cs_ref, w_ref, b_ref, o_ref):
    cs = cs_ref[...]
    a = cs * jax.nn.sigmoid(cs)
    o_ref[0] = jnp.dot(a, w_ref[0], preferred_element_type=F32,
                       precision=lax.Precision.HIGHEST) + b_ref[0]


def _ada(cs, ada_w, ada_b):
    depth, d, n = ada_w.shape
    tn = 1536
    return pl.pallas_call(
        _ada_kernel,
        grid=(depth, n // tn),
        in_specs=[pl.BlockSpec((SUBLANES, d), lambda l, j: (0, 0)),
                  pl.BlockSpec((1, d, tn), lambda l, j: (l, 0, j)),
                  pl.BlockSpec((1, 1, tn), lambda l, j: (l, 0, j))],
        out_specs=pl.BlockSpec((1, SUBLANES, tn), lambda l, j: (l, 0, j)),
        out_shape=jax.ShapeDtypeStruct((depth, SUBLANES, n), F32),
        compiler_params=_params("parallel", "parallel"),
        name="ada_mod",
    )(cs, ada_w, ada_b.reshape(depth, 1, n))


def _modulated_norm(x, gain, shift, scale):
    ms = jnp.mean(x * x, axis=-1, keepdims=True)
    return (x * lax.rsqrt(ms + EPS)) * gain * (1.0 + scale) + shift


def _proj_kernel(*refs, nq, nk, rope):
    if rope:
        (x_ref, mod_ref, g_ref, w_ref, gq_ref, gk_ref, bd_ref, cos_ref, sa_ref, sb_ref,
         q_ref, kt_ref, v_ref) = refs
    else:
        x_ref, mod_ref, g_ref, w_ref, gq_ref, gk_ref, bd_ref, q_ref, kt_ref, v_ref = refs
    h = _modulated_norm(x_ref[0], g_ref[...], mod_ref[0, 0:1, :], mod_ref[0, 1:2, :])
    p = jnp.dot(h.astype(BF16), w_ref[...], preferred_element_type=F32)
    bd = bd_ref[...]

    def head_norm(c0, gain):
        yc = p[:, c0:c0 + LANES]
        ss = jnp.dot((yc * yc).astype(BF16), bd, preferred_element_type=F32)
        yn = yc * lax.rsqrt(ss + EPS) * gain
        if rope:
            yn = (yn * cos_ref[...] + pltpu.roll(yn, LANES - 16, 1) * sa_ref[...]
                  + pltpu.roll(yn, 16, 1) * sb_ref[...])
        return yn

    for c in range(nq // LANES):
        c0 = c * LANES
        q_ref[0, :, c0:c0 + LANES] = head_norm(c0, gq_ref[:, c0:c0 + LANES]).astype(BF16)
    for c in range(nk // LANES):
        c0 = c * LANES
        yn = head_norm(nq + c0, gk_ref[:, c0:c0 + LANES])
        kt_ref[0, c0:c0 + LANES, :] = yn.T.astype(BF16)
    v_ref[0] = p[:, nq + nk:].astype(BF16)


def _proj(x, mod, gain, w, gq, gk, bd, rope_tabs, nq, nk, nv):
    b, l, d = x.shape
    tm = min(ROW_TILE, l)
    n = nq + nk + nv
    mod_map = (lambda i, j: (i, 0, 0)) if mod.shape[0] > 1 else (lambda i, j: (0, 0, 0))
    in_specs = [pl.BlockSpec((1, tm, d), lambda i, j: (i, j, 0)),
                pl.BlockSpec((1, 6, d), mod_map),
                pl.BlockSpec((1, d), lambda i, j: (0, 0)),
                pl.BlockSpec((d, n), lambda i, j: (0, 0)),
                pl.BlockSpec((1, nq), lambda i, j: (0, 0)),
                pl.BlockSpec((1, nk), lambda i, j: (0, 0)),
                pl.BlockSpec((LANES, LANES), lambda i, j: (0, 0))]
    args = [x, mod, gain, w, gq, gk, bd]
    if rope_tabs is not None:
        in_specs += [pl.BlockSpec((tm, LANES), lambda i, j: (j, 0))] * 3
        args += list(rope_tabs)
    return pl.pallas_call(
        functools.partial(_proj_kernel, nq=nq, nk=nk, rope=rope_tabs is not None),
        grid=(b, l // tm),
        in_specs=in_specs,
        out_specs=[pl.BlockSpec((1, tm, nq), lambda i, j: (i, j, 0)),
                   pl.BlockSpec((1, nk, tm), lambda i, j: (i, 0, j)),
                   pl.BlockSpec((1, tm, nv), lambda i, j: (i, j, 0))],
        out_shape=[jax.ShapeDtypeStruct((b, l, nq), BF16),
                   jax.ShapeDtypeStruct((b, nk, l), BF16),
                   jax.ShapeDtypeStruct((b, l, nv), BF16)],
        compiler_params=_params("parallel", "parallel"),
        name="in_proj",
    )(*args)


def _online_update(g, s, v, m_ref, l_ref, acc_ref):
    m_prev = m_ref[g]
    m_new = jnp.maximum(m_prev, jnp.max(s, axis=-1, keepdims=True))
    alpha = jnp.exp2(m_prev - m_new)
    p = jnp.exp2(s - m_new)
    l_ref[g] = alpha * l_ref[g] + jnp.sum(p, axis=-1, keepdims=True)
    acc_ref[g] = alpha * acc_ref[g] + jnp.dot(p.astype(BF16), v, preferred_element_type=F32)
    m_ref[g] = m_new


def _attn_sweep(kt_rows, v_lane, qs_ref, kt_ref, v_ref, s_ref, m_ref, l_ref, acc_ref):
    tq = qs_ref.shape[1] // len(kt_rows[0])
    m_ref[...] = jnp.full(m_ref.shape, -jnp.inf, F32)
    l_ref[...] = jnp.zeros(l_ref.shape, F32)
    acc_ref[...] = jnp.zeros(acc_ref.shape, F32)

    def scores(g, kt):
        rows = kt_rows[g]
        if len(set(rows)) == 1:
            return jnp.dot(qs_ref[g], kt(rows[0]), preferred_element_type=F32)
        return jnp.concatenate(
            [jnp.dot(qs_ref[g, i * tq:(i + 1) * tq, :], kt(r), preferred_element_type=F32)
             for i, r in enumerate(rows)], axis=0)

    tk = s_ref.shape[2]
    n_chunks = kt_ref.shape[2] // tk

    def kt_chunk(c):
        off = pl.multiple_of(c * tk, LANES)
        return lambda r: kt_ref[0, r:r + HEAD_DIM, pl.ds(off, tk)]

    def v_chunk(c, g):
        off = pl.multiple_of(c * tk, LANES)
        return v_ref[0, pl.ds(off, tk), v_lane[g]:v_lane[g] + LANES]

    s_ref[0] = scores(0, kt_chunk(0))

    def body(c, carry):
        s_ref[1] = scores(1, kt_chunk(c))
        _online_update(0, s_ref[0], v_chunk(c, 0), m_ref, l_ref, acc_ref)
        s_ref[0] = scores(0, kt_chunk(jnp.minimum(c + 1, n_chunks - 1)))
        _online_update(1, s_ref[1], v_chunk(c, 1), m_ref, l_ref, acc_ref)
        return carry

    lax.fori_loop(0, n_chunks, body, 0)


def _gqa_kernel(q_ref, kt_ref, v_ref, o_ref, qs_ref, m_ref, l_ref, acc_ref, s_ref):
    tq = q_ref.shape[1]
    group = A_HEADS // A_KV_HEADS
    for h in range(A_HEADS):
        qs_ref[h // group, (h % group) * tq:(h % group + 1) * tq, :] = (
            q_ref[0, :, h * HEAD_DIM:(h + 1) * HEAD_DIM])
    _attn_sweep([[g * HEAD_DIM] * group for g in range(A_KV_HEADS)], [0, 0],
                qs_ref, kt_ref, v_ref, s_ref, m_ref, l_ref, acc_ref)
    lane = lax.broadcasted_iota(jnp.int32, (tq, LANES), 1)
    for j in range(A_HEADS // 2):
        g, i = (2 * j) // group, (2 * j) % group
        rows_a = slice(i * tq, (i + 1) * tq)
        rows_c = slice((i + 1) * tq, (i + 2) * tq)
        a = acc_ref[g, rows_a, :] / l_ref[g, rows_a, :]
        c = acc_ref[g, rows_c, :] / l_ref[g, rows_c, :]
        if g == 0:
            c = pltpu.roll(c, HEAD_DIM, 1)
        else:
            a = pltpu.roll(a, HEAD_DIM, 1)
        o_ref[0, :, j * LANES:(j + 1) * LANES] = jnp.where(lane < HEAD_DIM, a, c).astype(BF16)


def _diff_kernel(q_ref, kt_ref, v_ref, lam_ref, sub_ref, o_ref, qs_ref, m_ref, l_ref, acc_ref, s_ref,
                 *, lam_init):
    tq = q_ref.shape[1]
    for g in range(2):
        for i in range(2):
            c0 = g * LANES + i * HEAD_DIM
            qs_ref[g, i * tq:(i + 1) * tq, :] = q_ref[0, :, c0:c0 + HEAD_DIM]
    _attn_sweep([[g * LANES, g * LANES + HEAD_DIM] for g in range(2)], [0, LANES],
                qs_ref, kt_ref, v_ref, s_ref, m_ref, l_ref, acc_ref)
    lf = lam_ref[...]
    lam = (jnp.exp(jnp.sum(lf[0:1] * lf[1:2], axis=-1, keepdims=True))
           - jnp.exp(jnp.sum(lf[2:3] * lf[3:4], axis=-1, keepdims=True)) + lam_init)
    for g in range(2):
        o = (acc_ref[g, 0:tq, :] / l_ref[g, 0:tq, :]
             - lam * (acc_ref[g, tq:2 * tq, :] / l_ref[g, tq:2 * tq, :]))
        ms = jnp.mean(o * o, axis=-1, keepdims=True)
        o = (o * lax.rsqrt(ms + EPS)) * sub_ref[...] * (1.0 - lam_init)
        o_ref[0, :, g * LANES:(g + 1) * LANES] = o.astype(BF16)


def _attn_scratch(rows, tk):
    return [pltpu.VMEM((2, rows, HEAD_DIM), BF16), pltpu.VMEM((2, rows, 1), F32),
            pltpu.VMEM((2, rows, 1), F32), pltpu.VMEM((2, rows, LANES), F32),
            pltpu.VMEM((2, rows, tk), F32)]


def _key_chunk(n_keys, rows):
    cap = min(SCORE_ELEMS // rows, n_keys)
    return max(c for c in range(LANES, cap + 1, LANES) if n_keys % c == 0)


_KV_A_BLOCK = B_HEADS * 2 * HEAD_DIM // LANES


def _gqa(q, kt, v):
    b, lq, _ = q.shape
    tq = min(Q_TILE, lq)
    n_keys = kt.shape[2]
    blk = _KV_A_BLOCK
    return pl.pallas_call(
        _gqa_kernel,
        grid=(b, lq // tq),
        in_specs=[pl.BlockSpec((1, tq, A_HEADS * HEAD_DIM), lambda i, j: (i, j, 0)),
                  pl.BlockSpec((1, LANES, n_keys), lambda i, j: (i, blk, 0)),
                  pl.BlockSpec((1, n_keys, LANES), lambda i, j: (i, 0, blk))],
        out_specs=pl.BlockSpec((1, tq, A_HEADS * HEAD_DIM), lambda i, j: (i, j, 0)),
        out_shape=jax.ShapeDtypeStruct((b, lq, A_HEADS * HEAD_DIM), BF16),
        scratch_shapes=_attn_scratch(A_HEADS // A_KV_HEADS * tq,
                                     _key_chunk(n_keys, A_HEADS // A_KV_HEADS * tq)),
        compiler_params=_params("parallel", "parallel"),
        name="gqa_attn",
    )(q, kt, v)


def _diff(q, kt, v, b_lambda, b_subln, lam_init):
    b, lq, _ = q.shape
    tq = min(Q_TILE, lq)
    n_keys = kt.shape[2]
    w = 2 * LANES
    q0 = A_HEADS * HEAD_DIM // w
    return pl.pallas_call(
        functools.partial(_diff_kernel, lam_init=lam_init),
        grid=(b, B_HEADS // 2, lq // tq),
        in_specs=[pl.BlockSpec((1, tq, w), lambda i, h, j: (i, j, q0 + h)),
                  pl.BlockSpec((1, w, n_keys), lambda i, h, j: (i, h, 0)),
                  pl.BlockSpec((1, n_keys, w), lambda i, h, j: (i, 0, h)),
                  pl.BlockSpec((4, HEAD_DIM), lambda i, h, j: (0, 0)),
                  pl.BlockSpec((1, LANES), lambda i, h, j: (0, 0))],
        out_specs=pl.BlockSpec((1, tq, w), lambda i, h, j: (i, j, h)),
        out_shape=jax.ShapeDtypeStruct((b, lq, B_HEADS * LANES), BF16),
        scratch_shapes=_attn_scratch(2 * tq, _key_chunk(n_keys, 2 * tq)),
        compiler_params=_params("parallel", "parallel", "parallel"),
        name="diff_attn",
    )(q, kt, v, b_lambda, b_subln)


def _na_kernel(q_ref, km_ref, k0_ref, kp_ref, kc_ref, vm_ref, v0_ref, vp_ref, vc_ref, bias_ref,
               o_ref, *, rows):
    i = pl.program_id(1)
    tq = q_ref.shape[1]
    nb = 3 * tq
    qrow = NA_BLOCK_ROWS * i + lax.broadcasted_iota(jnp.int32, (tq, nb), 0) // GRID_W
    krow = NA_BLOCK_ROWS * (i - 1) + lax.broadcasted_iota(jnp.int32, (tq, nb), 1) // GRID_W
    rs = jnp.clip(qrow - NA_ROWS // 2, 0, rows - NA_ROWS)
    row_mask = jnp.where((krow >= rs) & (krow < rs + NA_ROWS), 0.0, NEG).astype(F32)
    lane = lax.broadcasted_iota(jnp.int32, (tq, LANES), 1)
    kts = (km_ref, k0_ref, kp_ref)
    vs = (vm_ref, v0_ref, vp_ref)
    for pair in range(C_HEADS // 2):
        outs = []
        for h in (2 * pair, 2 * pair + 1):
            r0 = h * HEAD_DIM
            q = q_ref[0, :, r0:r0 + HEAD_DIM]
            s_band = jnp.concatenate(
                [jnp.dot(q, k[0, r0:r0 + HEAD_DIM, :], preferred_element_type=F32) for k in kts],
                axis=1) + bias_ref[h] + row_mask
            s_ctx = jnp.dot(q, kc_ref[0, r0:r0 + HEAD_DIM, :], preferred_element_type=F32)
            m = jnp.maximum(jnp.max(s_band, axis=-1, keepdims=True),
                            jnp.max(s_ctx, axis=-1, keepdims=True))
            p_band = jnp.exp2(s_band - m)
            p_ctx = jnp.exp2(s_ctx - m)
            denom = jnp.sum(p_band, axis=-1, keepdims=True) + jnp.sum(p_ctx, axis=-1, keepdims=True)
            c0 = pair * LANES
            pv = jnp.dot(p_ctx.astype(BF16), vc_ref[0, :, c0:c0 + LANES], preferred_element_type=F32)
            for t, v in enumerate(vs):
                pv += jnp.dot(p_band[:, t * tq:(t + 1) * tq].astype(BF16), v[0, :, c0:c0 + LANES],
                              preferred_element_type=F32)
            outs.append(pv / denom)
        a, c = outs
        o_ref[0, :, pair * LANES:(pair + 1) * LANES] = jnp.where(lane < HEAD_DIM, a, c).astype(BF16)


def _na(q, kt_lat, v_lat, kt_ctx, v_ctx, bias):
    b, s, w = q.shape
    tq = NA_BLOCK_ROWS * GRID_W
    nblk = s // tq
    nc = kt_ctx.shape[2]
    rows = s // GRID_W

    def kt_spec(d):
        return pl.BlockSpec((1, w, tq), lambda i, j: (i, 0, jnp.clip(j + d, 0, nblk - 1)))

    def v_spec(d):
        return pl.BlockSpec((1, tq, w), lambda i, j: (i, jnp.clip(j + d, 0, nblk - 1), 0))

    return pl.pallas_call(
        functools.partial(_na_kernel, rows=rows),
        grid=(b, nblk),
        in_specs=[pl.BlockSpec((1, tq, w), lambda i, j: (i, j, 0)),
                  kt_spec(-1), kt_spec(0), kt_spec(1),
                  pl.BlockSpec((1, w, nc), lambda i, j: (i, 0, 0)),
                  v_spec(-1), v_spec(0), v_spec(1),
                  pl.BlockSpec((1, nc, w), lambda i, j: (i, 0, 0)),
                  pl.BlockSpec(bias.shape, lambda i, j: (0, 0, 0))],
        out_specs=pl.BlockSpec((1, tq, w), lambda i, j: (i, j, 0)),
        out_shape=jax.ShapeDtypeStruct((b, s, w), BF16),
        compiler_params=_params("parallel", "parallel"),
        name="na_attn",
    )(q, kt_lat, kt_lat, kt_lat, kt_ctx, v_lat, v_lat, v_lat, v_ctx, bias)


def _na_bias_table(rpb):
    heads = rpb.shape[0]
    nr, nc = 2 * NA_ROWS - 1, 2 * NA_COLS - 1
    j = np.arange(GRID_W)
    dc = j[None, :] - j[:, None] + (NA_COLS - 1)
    cs = np.clip(j - NA_COLS // 2, 0, GRID_W - NA_COLS)
    col_ok = (j[None, :] >= cs[:, None]) & (j[None, :] < cs[:, None] + NA_COLS)
    onehot = (np.arange(nc)[:, None, None] == dc[None]) & col_ok[None]
    cols = jnp.einsum('hrm,mjk->hrjk', rpb, jnp.asarray(onehot, F32),
                      precision=lax.Precision.HIGHEST)
    cols = jnp.where(col_ok[None, None], cols, NEG)
    band = 3 * NA_BLOCK_ROWS
    per_q = [cols[:, NA_ROWS - 1 - NA_BLOCK_ROWS - qr:NA_ROWS - 1 - NA_BLOCK_ROWS - qr + band]
             for qr in range(NA_BLOCK_ROWS)]
    t = jnp.stack(per_q, axis=1)
    assert t.shape == (heads, NA_BLOCK_ROWS, band, GRID_W, GRID_W) and nr == band + NA_BLOCK_ROWS - 1
    return t.transpose(0, 1, 3, 2, 4).reshape(heads, NA_BLOCK_ROWS * GRID_W, band * GRID_W)


def _out_kernel(o1_ref, o2_ref, w_ref, x_ref, mod_ref, g_ref, x1_ref, ht_ref):
    half = o1_ref.shape[2]
    o = (jnp.dot(o1_ref[0], w_ref[0:half, :], preferred_element_type=F32)
         + jnp.dot(o2_ref[0], w_ref[half:, :], preferred_element_type=F32))
    x1 = x_ref[0] + mod_ref[0, 2:3, :] * o
    x1_ref[0] = x1
    h2 = _modulated_norm(x1, g_ref[...], mod_ref[0, 3:4, :], mod_ref[0, 4:5, :])
    ht_ref[0] = pltpu.bitcast(h2.T.astype(BF16), jnp.int32)


def _out(o1, o1_blk, o2, o2_blk, w, x, mod, gain):
    b, l, d = x.shape
    tm = min(ROW_TILE, l)
    half = d // 2
    mod_map = (lambda i, j: (i, 0, 0)) if mod.shape[0] > 1 else (lambda i, j: (0, 0, 0))
    return pl.pallas_call(
        _out_kernel,
        grid=(b, l // tm),
        in_specs=[pl.BlockSpec((1, tm, half), lambda i, j: (i, j, o1_blk)),
                  pl.BlockSpec((1, tm, half), lambda i, j: (i, j, o2_blk)),
                  pl.BlockSpec((d, d), lambda i, j: (0, 0)),
                  pl.BlockSpec((1, tm, d), lambda i, j: (i, j, 0)),
                  pl.BlockSpec((1, 6, d), mod_map),
                  pl.BlockSpec((1, d), lambda i, j: (0, 0))],
        out_specs=[pl.BlockSpec((1, tm, d), lambda i, j: (i, j, 0)),
                   pl.BlockSpec((1, d // 2, tm), lambda i, j: (i, 0, j))],
        out_shape=[jax.ShapeDtypeStruct((b, l, d), F32),
                   jax.ShapeDtypeStruct((b, d // 2, l), jnp.int32)],
        compiler_params=_params("parallel", "parallel"),
        name="out_proj",
    )(o1, o2, w, x, mod, gain)


def _top16(s, iota, exact):
    rank = jnp.full(s.shape, float(PEER_TOPK), F32)
    big = float(s.shape[0])
    vals = []
    for r in range(PEER_TOPK):
        m = jnp.max(s, axis=0, keepdims=True)
        hit = s == m
        if exact:
            hit = iota == jnp.min(jnp.where(hit, iota, big), axis=0, keepdims=True)
        rank = jnp.where(hit, float(r), rank)
        s = jnp.where(hit, -jnp.inf, s)
        vals.append(m)
    return vals, rank


def _rows_tile(rows, iota8):
    t = jnp.zeros((SUBLANES,) + rows[0].shape[1:], F32)
    for k, r in enumerate(rows):
        t = jnp.where(iota8 == k, r, t)
    return t


_CAND_COUNTS = [PEER_TOPK // (a + 1) for a in range(SUBLANES)]


def _retrieve(s1, s2, exact):
    iota = lax.broadcasted_iota(jnp.int32, (PEER_N_KEYS, LANES), 0).astype(F32)
    iota8 = lax.broadcasted_iota(jnp.int32, (SUBLANES, LANES), 0)
    iota_c = lax.broadcasted_iota(jnp.int32, (SUBLANES * (SUBLANES + 2), LANES), 0).astype(F32)
    v1, r1 = _top16(s1, iota, exact)
    v2, r2 = _top16(s2, iota, exact)
    e1 = [jnp.exp(v - v1[0]) for v in v1]
    e2 = [jnp.exp(v - v2[0]) for v in v2]
    v2_lo, v2_hi = _rows_tile(v2[:8], iota8), _rows_tile(v2[8:], iota8)
    e2_lo, e2_hi = _rows_tile(e2[:8], iota8), _rows_tile(e2[8:], iota8)
    v1_hi, e1_hi = _rows_tile(v1[8:], iota8), _rows_tile(e1[8:], iota8)
    cand, ecand = [], []
    for a, cnt in enumerate(_CAND_COUNTS):
        cand.append(jnp.where(iota8 < cnt, v1[a] + v2_lo, -jnp.inf))
        ecand.append(e1[a] * e2_lo)
        if a == 0:
            cand.append(v1[0] + v2_hi)
            ecand.append(e1[0] * e2_hi)
    cand.append(v1_hi + v2[0])
    ecand.append(e1_hi * e2[0])
    cand = jnp.concatenate(cand, axis=0)
    ecand = jnp.concatenate(ecand, axis=0)
    _, rc = _top16(cand, iota_c, exact)
    sel = (rc < float(PEER_TOPK)).astype(F32)
    ranked = (jnp.sum(sel, axis=0, keepdims=True)
              + jnp.sum((r1 < float(PEER_TOPK)).astype(F32), axis=0, keepdims=True)
              + jnp.sum((r2 < float(PEER_TOPK)).astype(F32), axis=0, keepdims=True))
    z = jnp.sum(sel * ecand, axis=0, keepdims=True)
    n_rows = [jnp.sum(sel[0:2 * SUBLANES], axis=0, keepdims=True)]
    for a in range(1, SUBLANES):
        n_rows.append(jnp.sum(sel[SUBLANES * (a + 1):SUBLANES * (a + 2)], axis=0, keepdims=True))
    n_hi = sel[SUBLANES * (SUBLANES + 1):]
    n1 = jnp.zeros(s1.shape, F32)
    for a in range(PEER_TOPK):
        na = n_rows[a] if a < SUBLANES else n_hi[a - SUBLANES:a - SUBLANES + 1]
        n1 = jnp.where(r1 == float(a), na, n1)
    return n1, jnp.exp(s1 - v1[0]) / z, r2, jnp.exp(s2 - v2[0]), ranked


def _topk_kernel(ht_ref, wq_ref, keys_ref, n1_ref, c1_ref, b2_ref, e2_ref, qt_ref, s_ref):
    chunks = ht_ref.shape[2] // LANES
    qt_ref[...] = jnp.dot(wq_ref[...], pltpu.bitcast(ht_ref[0], BF16),
                          preferred_element_type=F32).astype(BF16)
    for hp in range(2 * PEER_HEADS):
        s_ref[hp] = jnp.dot(keys_ref[hp], qt_ref[hp * PEER_N_KEYS:(hp + 1) * PEER_N_KEYS, :],
                            preferred_element_type=F32)

    def emit(h, sl, exact):
        n1, c1, b2, e2, ranked = _retrieve(s_ref[2 * h, :, sl], s_ref[2 * h + 1, :, sl], exact)
        n1_ref[0, h, :, sl] = n1
        c1_ref[0, h, :, sl] = c1
        b2_ref[0, h, :, sl] = pltpu.bitcast(b2.astype(BF16), jnp.int32)
        e2_ref[0, h, :, sl] = pltpu.bitcast(e2.astype(BF16), jnp.int32)
        return ranked

    def pair(i, carry):
        h = i // (chunks // 2)
        c0 = (i % (chunks // 2)) * 2 * LANES
        sls = [pl.ds(pl.multiple_of(c0 + j * LANES, LANES), LANES) for j in range(2)]
        ranked = jnp.maximum(emit(h, sls[0], False), emit(h, sls[1], False))

        @pl.when(jnp.max(ranked) > float(3 * PEER_TOPK))
        def _():
            for sl in sls:
                emit(h, sl, True)

        return carry

    lax.fori_loop(0, PEER_HEADS * chunks // 2, pair, 0)


def _peer_topk(ht, wq_t, keys, tt):
    b, d2, l = ht.shape
    nqd = wq_t.shape[0]
    shp = (b, PEER_HEADS, PEER_N_KEYS, l)
    shp_packed = (b, PEER_HEADS, PEER_N_KEYS // 2, l)
    ospec = pl.BlockSpec((1, PEER_HEADS, PEER_N_KEYS, tt), lambda i, j: (i, 0, 0, j))
    ospec_packed = pl.BlockSpec((1, PEER_HEADS, PEER_N_KEYS // 2, tt), lambda i, j: (i, 0, 0, j))
    return pl.pallas_call(
        _topk_kernel,
        grid=(b, l // tt),
        in_specs=[pl.BlockSpec((1, d2, tt), lambda i, j: (i, 0, j)),
                  pl.BlockSpec((nqd, 2 * d2), lambda i, j: (0, 0)),
                  pl.BlockSpec(keys.shape, lambda i, j: (0, 0, 0))],
        out_specs=[ospec, ospec, ospec_packed, ospec_packed],
        out_shape=[jax.ShapeDtypeStruct(shp, F32), jax.ShapeDtypeStruct(shp, F32),
                   jax.ShapeDtypeStruct(shp_packed, jnp.int32),
                   jax.ShapeDtypeStruct(shp_packed, jnp.int32)],
        scratch_shapes=[pltpu.VMEM((nqd, tt), BF16),
                        pltpu.VMEM((2 * PEER_HEADS, PEER_N_KEYS, tt), F32)],
        compiler_params=_params("parallel", "parallel"),
        name="peer_topk",
    )(ht, wq_t, keys)


def _peer_kernel(ht_ref, n1_ref, c1_ref, b2_ref, e2_ref, u_ref, vt_ref, x_ref, mod_ref, o_ref,
                 a_ref, g_ref, acc_ref):
    e = pl.program_id(2)
    et = 2 * u_ref.shape[0]
    ht = pltpu.bitcast(ht_ref[0], BF16)

    @pl.when(e == 0)
    def _():
        acc_ref[...] = jnp.zeros(acc_ref.shape, F32)

    a_ref[...] = jnp.dot(pltpu.bitcast(u_ref[...], BF16), ht, preferred_element_type=F32)
    tile3 = (PEER_N_KEYS // BF16_ROWS, BF16_ROWS, LANES)
    for k in range(et // PEER_N_KEYS):
        rows = slice(k * PEER_N_KEYS, (k + 1) * PEER_N_KEYS)
        for c in range(ht.shape[1] // LANES):
            sl = slice(c * LANES, (c + 1) * LANES)
            w = jnp.zeros(tile3, BF16)
            for h in range(PEER_HEADS):
                n_t = jnp.broadcast_to(n1_ref[0, h, k:k + 1, sl], tile3[1:]).astype(BF16)[None]
                c_t = jnp.broadcast_to(c1_ref[0, h, k:k + 1, sl], tile3[1:]).astype(BF16)[None]
                b2 = pltpu.bitcast(b2_ref[0, h, :, sl], BF16).reshape(tile3)
                e2 = pltpu.bitcast(e2_ref[0, h, :, sl], BF16).reshape(tile3)
                w = w + jnp.clip(n_t - b2, 0, 1) * (e2 * c_t)
            a = a_ref[rows, sl]
            z = a * (a * a * (-2.0 * _GELU_K * 0.044715 * _LOG2E) - 2.0 * _GELU_K * _LOG2E)
            act = (a / (1.0 + jnp.exp2(z))).astype(BF16)
            g_ref[rows, sl] = (w * act.reshape(tile3)).reshape(PEER_N_KEYS, LANES)
    acc_ref[...] += jnp.dot(pltpu.bitcast(vt_ref[...], BF16), g_ref[...],
                            preferred_element_type=F32)

    @pl.when(e == pl.num_programs(2) - 1)
    def _():
        o_ref[0] = x_ref[0] + mod_ref[0, 5:6, :] * acc_ref[...].T


def _peer_dense(ht, n1, c1, b2, e2, u, vt, x, mod, tt):
    b, d2, l = ht.shape
    d = 2 * d2
    ne = 2 * u.shape[0]
    et = PEER_EXPERTS
    mod_map = (lambda i, j, k: (i, 0, 0)) if mod.shape[0] > 1 else (lambda i, j, k: (0, 0, 0))
    sel_spec = pl.BlockSpec((1, PEER_HEADS, PEER_N_KEYS // 2, tt), lambda i, j, k: (i, 0, 0, j))
    row_spec = pl.BlockSpec((1, PEER_HEADS, et // PEER_N_KEYS, tt), lambda i, j, k: (i, 0, k, j))
    return pl.pallas_call(
        _peer_kernel,
        grid=(b, l // tt, ne // et),
        in_specs=[pl.BlockSpec((1, d2, tt), lambda i, j, k: (i, 0, j)),
                  row_spec, row_spec, sel_spec, sel_spec,
                  pl.BlockSpec((et // 2, d), lambda i, j, k: (k, 0)),
                  pl.BlockSpec((d2, et), lambda i, j, k: (0, k)),
                  pl.BlockSpec((1, tt, d), lambda i, j, k: (i, j, 0)),
                  pl.BlockSpec((1, 6, d), mod_map)],
        out_specs=pl.BlockSpec((1, tt, d), lambda i, j, k: (i, j, 0)),
        out_shape=jax.ShapeDtypeStruct((b, l, d), F32),
        scratch_shapes=[pltpu.VMEM((et, tt), F32), pltpu.VMEM((et, tt), BF16),
                        pltpu.VMEM((d, tt), F32)],
        compiler_params=_params("parallel", "parallel", "arbitrary"),
        name="peer_mix",
    )(ht, n1, c1, b2, e2, u, vt, x, mod)


def _peer(ht, x1, mod, wq_t, keys, u, vt):
    tt = min(PEER_TOKENS, ht.shape[2])
    n1, c1, b2, e2 = _peer_topk(ht, wq_t, keys, tt)
    return _peer_dense(ht, n1, c1, b2, e2, u, vt, x1, mod, tt)


def _rope_tables(s):
    t = np.arange(s)
    pos = np.stack([t // GRID_W, t % GRID_W], axis=1).astype(np.float32)
    axis_dim = HEAD_DIM // 2
    freqs = 1.0 / (ROPE_THETA ** (jnp.arange(0, axis_dim, 2, dtype=F32) / axis_dim))
    ang = jnp.asarray(pos)[:, :, None] * freqs[None, None, :]
    cos, sin = jnp.cos(ang), jnp.sin(ang)
    zero = jnp.zeros_like(sin)
    def lanes(first_half, second_half):
        p = jnp.stack([first_half, second_half], axis=2).reshape(s, HEAD_DIM)
        return jnp.concatenate([p, p], axis=1)
    return lanes(cos, cos), lanes(-sin, zero), lanes(zero, sin)


def _pack_rows(x):
    bits = lax.bitcast_convert_type(x, jnp.uint16).astype(jnp.uint32)
    return lax.bitcast_convert_type(bits[0::2] | (bits[1::2] << 16), jnp.int32)


def _block_diag_mean():
    i = np.arange(LANES)
    return jnp.asarray((i[:, None] // HEAD_DIM == i[None, :] // HEAD_DIM) / HEAD_DIM, BF16)


def kernel(x, c, ctx, c_ctx, ada_w, ada_b, norm_g, ab_w_in, ab_w_out, a_q_norm, a_k_norm, b_q_norm,
           b_k_norm, b_lambda, b_subln, c_w_in, c_w_out, c_q_norm, c_k_norm, c_rpb, peer_w_q,
           peer_keys, peer_u, peer_v):
    bsz, s, d = x.shape
    depth = ada_w.shape[0]
    assert depth == 2, "one attention-pair layer followed by one (final) neighbourhood layer"
    log2e = math.log2(math.e)
    scale = HEAD_DIM ** -0.5 * log2e

    cs = jnp.zeros((SUBLANES, d), F32).at[:bsz].set(c).at[bsz].set(c_ctx)
    mods = _ada(cs, ada_w, ada_b)
    bd = _block_diag_mean()
    rope = _rope_tables(s)

    for l in range(depth):
        last = l == depth - 1
        mod_lat = mods[l, :bsz].reshape(bsz, 6, d)
        mod_ctx = mods[l, bsz:bsz + 1].reshape(1, 6, d)
        g1 = norm_g[l, 0].reshape(1, d)
        g2 = norm_g[l, 1].reshape(1, d)
        i = l // 2
        wq_t = peer_w_q[l].T.astype(BF16)
        keys = peer_keys[l].reshape(2 * PEER_HEADS, PEER_N_KEYS, -1).astype(BF16)
        u = _pack_rows(peer_u[l].astype(BF16))
        vt = _pack_rows(peer_v[l].T.astype(BF16))
        if l % 2 == 0:
            lam_init = 0.8 - 0.6 * math.exp(-0.3 * l)
            qa, ka, va, qb, kb, vb = jnp.split(
                ab_w_in[i], np.cumsum([512, 128, 128, 512, 512, 512])[:-1].tolist(), axis=1)
            w = jnp.concatenate([qa, qb, kb, ka, vb, va], axis=1).astype(BF16)
            nq, nk, nv = 1024, 640, 640
            gq = jnp.concatenate([jnp.tile(a_q_norm[i], A_HEADS), jnp.tile(b_q_norm[i], 2 * B_HEADS)])
            gk = jnp.concatenate([jnp.tile(b_k_norm[i], 2 * B_HEADS), jnp.tile(a_k_norm[i], A_KV_HEADS)])
            gq = (gq * scale).reshape(1, nq)
            gk = gk.reshape(1, nk)
            q_l, kt_l, v_l = _proj(x, mod_lat, g1, w, gq, gk, bd, rope, nq, nk, nv)
            q_c, kt_c, v_c = _proj(ctx, mod_ctx, g1, w, gq, gk, bd, None, nq, nk, nv)
            lam2 = b_lambda[i]
            sub = b_subln[i].reshape(1, LANES)
            kt_all = jnp.concatenate([kt_l, kt_c], axis=2)
            v_all = jnp.concatenate([v_l, v_c], axis=1)
            oa = _gqa(q_l, kt_all, v_all)
            ob = _diff(q_l, kt_all, v_all, lam2, sub, lam_init)
            w_out = ab_w_out[i].astype(BF16)
            x1, ht = _out(oa, 0, ob, 0, w_out, x, mod_lat, g2)
            if not last:
                oa_c = _gqa(q_c, kt_c, v_c)
                ob_c = _diff(q_c, kt_c, v_c, lam2, sub, lam_init)
                ctx1, ht_c = _out(oa_c, 0, ob_c, 0, w_out, ctx, mod_ctx, g2)
        else:
            w = c_w_in[i].astype(BF16)
            nq = nk = nv = C_HEADS * HEAD_DIM
            gq = (jnp.tile(c_q_norm[i], C_HEADS) * scale).reshape(1, nq)
            gk = jnp.tile(c_k_norm[i], C_HEADS).reshape(1, nk)
            q_l, kt_l, v_l = _proj(x, mod_lat, g1, w, gq, gk, bd, None, nq, nk, nv)
            q_c, kt_c, v_c = _proj(ctx, mod_ctx, g1, w, gq, gk, bd, None, nq, nk, nv)
            o = _na(q_l, kt_l, v_l, kt_c, v_c, _na_bias_table(c_rpb[i] * log2e))
            w_out = c_w_out[i].astype(BF16)
            x1, ht = _out(o, 0, o, 1, w_out, x, mod_lat, g2)
        x = _peer(ht, x1, mod_lat, wq_t, keys, u, vt)
        if not last:
            ctx = _peer(ht_c, ctx1, mod_ctx, wq_t, keys, u, vt)
    return x
```

```python
import functools
import math

import numpy as np
import jax
import jax.numpy as jnp
from jax import lax
from jax.experimental import pallas as pl
from jax.experimental.pallas import tpu as pltpu

F32 = jnp.float32
BF16 = jnp.bfloat16

D_MODEL = 1024
GRID_W = 64
HEAD_DIM = 64
ROPE_THETA = 10000.0
EPS = 1e-6
A_HEADS = 8
A_KV_HEADS = 2
B_HEADS = 4
C_HEADS = 16
NA_ROWS = 8
NA_COLS = 16
PEER_HEADS = 8
PEER_N_KEYS = 128
PEER_TOPK = 16

LANES = 128
SUBLANES = 8
BF16_ROWS = 16
VMEM_LIMIT = 56 * 1024 * 1024
NEG = -1e30

ROW_TILE = 512
Q_TILE = 256
SCORE_ELEMS = 1536 * 1024
NA_BLOCK_ROWS = 4
PEER_TOKENS = 512
PEER_EXPERTS = 1024
_GELU_K = math.sqrt(2.0 / math.pi)
_LOG2E = math.log2(math.e)


def _params(*sem):
    return pltpu.CompilerParams(dimension_semantics=sem, vmem_limit_bytes=VMEM_LIMIT)


def _ada_kernel(cs_ref, w_ref, b_ref, o_ref):
    cs = cs_ref[...]
    a = cs * jax.nn.sigmoid(cs)
    o_ref[0] = jnp.dot(a, w_ref[0], preferred_element_type=F32,
                       precision=lax.Precision.HIGHEST) + b_ref[0]


def _ada(cs, ada_w, ada_b):
    depth, d, n = ada_w.shape
    tn = 1536
    return pl.pallas_call(
        _ada_kernel,
        grid=(depth, n // tn),
        in_specs=[pl.BlockSpec((SUBLANES, d), lambda l, j: (0, 0)),
                  pl.BlockSpec((1, d, tn), lambda l, j: (l, 0, j)),
                  pl.BlockSpec((1, 1, tn), lambda l, j: (l, 0, j))],
        out_specs=pl.BlockSpec((1, SUBLANES, tn), lambda l, j: (l, 0, j)),
        out_shape=jax.ShapeDtypeStruct((depth, SUBLANES, n), F32),
        compiler_params=_params("parallel", "parallel"),
        name="ada_mod",
    )(cs, ada_w, ada_b.reshape(depth, 1, n))


def _modulated_norm(x, gain, shift, scale):
    ms = jnp.mean(x * x, axis=-1, keepdims=True)
    return (x * lax.rsqrt(ms + EPS)) * gain * (1.0 + scale) + shift


def _proj_kernel(*refs, nq, nk, rope):
    if rope:
        (x_ref, mod_ref, g_ref, w_ref, gq_ref, gk_ref, bd_ref, cos_ref, sa_ref, sb_ref,
         q_ref, kt_ref, v_ref) = refs
    else:
        x_ref, mod_ref, g_ref, w_ref, gq_ref, gk_ref, bd_ref, q_ref, kt_ref, v_ref = refs
    h = _modulated_norm(x_ref[0], g_ref[...], mod_ref[0, 0:1, :], mod_ref[0, 1:2, :])
    p = jnp.dot(h.astype(BF16), w_ref[...], preferred_element_type=F32)
    bd = bd_ref[...]

    def head_norm(c0, gain):
        yc = p[:, c0:c0 + LANES]
        ss = jnp.dot((yc * yc).astype(BF16), bd, preferred_element_type=F32)
        yn = yc * lax.rsqrt(ss + EPS) * gain
        if rope:
            yn = (yn * cos_ref[...] + pltpu.roll(yn, LANES - 16, 1) * sa_ref[...]
                  + pltpu.roll(yn, 16, 1) * sb_ref[...])
        return yn

    for c in range(nq // LANES):
        c0 = c * LANES
        q_ref[0, :, c0:c0 + LANES] = head_norm(c0, gq_ref[:, c0:c0 + LANES]).astype(BF16)
    for c in range(nk // LANES):
        c0 = c * LANES
        yn = head_norm(nq + c0, gk_ref[:, c0:c0 + LANES])
        kt_ref[0, c0:c0 + LANES, :] = yn.T.astype(BF16)
    v_ref[0] = p[:, nq + nk:].astype(BF16)


def _proj(x, mod, gain, w, gq, gk, bd, rope_tabs, nq, nk, nv):
    b, l, d = x.shape
    tm = min(ROW_TILE, l)
    n = nq + nk + nv
    mod_map = (lambda i, j: (i, 0, 0)) if mod.shape[0] > 1 else (lambda i, j: (0, 0, 0))
    in_specs = [pl.BlockSpec((1, tm, d), lambda i, j: (i, j, 0)),
                pl.BlockSpec((1, 6, d), mod_map),
                pl.BlockSpec((1, d), lambda i, j: (0, 0)),
                pl.BlockSpec((d, n), lambda i, j: (0, 0)),
                pl.BlockSpec((1, nq), lambda i, j: (0, 0)),
                pl.BlockSpec((1, nk), lambda i, j: (0, 0)),
                pl.BlockSpec((LANES, LANES), lambda i, j: (0, 0))]
    args = [x, mod, gain, w, gq, gk, bd]
    if rope_tabs is not None:
        in_specs += [pl.BlockSpec((tm, LANES), lambda i, j: (j, 0))] * 3
        args += list(rope_tabs)
    return pl.pallas_call(
        functools.partial(_proj_kernel, nq=nq, nk=nk, rope=rope_tabs is not None),
        grid=(b, l // tm),
        in_specs=in_specs,
        out_specs=[pl.BlockSpec((1, tm, nq), lambda i, j: (i, j, 0)),
                   pl.BlockSpec((1, nk, tm), lambda i, j: (i, 0, j)),
                   pl.BlockSpec((1, tm, nv), lambda i, j: (i, j, 0))],
        out_shape=[jax.ShapeDtypeStruct((b, l, nq), BF16),
                   jax.ShapeDtypeStruct((b, nk, l), BF16),
                   jax.ShapeDtypeStruct((b, l, nv), BF16)],
        compiler_params=_params("parallel", "parallel"),
        name="in_proj",
    )(*args)


def _online_update(g, s, v, m_ref, l_ref, acc_ref):
    m_prev = m_ref[g]
    m_new = jnp.maximum(m_prev, jnp.max(s, axis=-1, keepdims=True))
    alpha = jnp.exp2(m_prev - m_new)
    p = jnp.exp2(s - m_new)
    l_ref[g] = alpha * l_ref[g] + jnp.sum(p, axis=-1, keepdims=True)
    acc_ref[g] = alpha * acc_ref[g] + jnp.dot(p.astype(BF16), v, preferred_element_type=F32)
    m_ref[g] = m_new


def _attn_sweep(kt_rows, v_lane, qs_ref, kt_ref, v_ref, s_ref, m_ref, l_ref, acc_ref):
    tq = qs_ref.shape[1] // len(kt_rows[0])
    m_ref[...] = jnp.full(m_ref.shape, -jnp.inf, F32)
    l_ref[...] = jnp.zeros(l_ref.shape, F32)
    acc_ref[...] = jnp.zeros(acc_ref.shape, F32)

    def scores(g, kt):
        rows = kt_rows[g]
        if len(set(rows)) == 1:
            return jnp.dot(qs_ref[g], kt(rows[0]), preferred_element_type=F32)
        return jnp.concatenate(
            [jnp.dot(qs_ref[g, i * tq:(i + 1) * tq, :], kt(r), preferred_element_type=F32)
             for i, r in enumerate(rows)], axis=0)

    tk = s_ref.shape[2]
    n_chunks = kt_ref.shape[2] // tk

    def kt_chunk(c):
        off = pl.multiple_of(c * tk, LANES)
        return lambda r: kt_ref[0, r:r + HEAD_DIM, pl.ds(off, tk)]

    def v_chunk(c, g):
        off = pl.multiple_of(c * tk, LANES)
        return v_ref[0, pl.ds(off, tk), v_lane[g]:v_lane[g] + LANES]

    s_ref[0] = scores(0, kt_chunk(0))

    def body(c, carry):
        s_ref[1] = scores(1, kt_chunk(c))
        _online_update(0, s_ref[0], v_chunk(c, 0), m_ref, l_ref, acc_ref)
        s_ref[0] = scores(0, kt_chunk(jnp.minimum(c + 1, n_chunks - 1)))
        _online_update(1, s_ref[1], v_chunk(c, 1), m_ref, l_ref, acc_ref)
        return carry

    lax.fori_loop(0, n_chunks, body, 0)


def _gqa_kernel(q_ref, kt_ref, v_ref, o_ref, qs_ref, m_ref, l_ref, acc_ref, s_ref):
    tq = q_ref.shape[1]
    group = A_HEADS // A_KV_HEADS
    for h in range(A_HEADS):
        qs_ref[h // group, (h % group) * tq:(h % group + 1) * tq, :] = (
            q_ref[0, :, h * HEAD_DIM:(h + 1) * HEAD_DIM])
    _attn_sweep([[g * HEAD_DIM] * group for g in range(A_KV_HEADS)], [0, 0],
                qs_ref, kt_ref, v_ref, s_ref, m_ref, l_ref, acc_ref)
    lane = lax.broadcasted_iota(jnp.int32, (tq, LANES), 1)
    for j in range(A_HEADS // 2):
        g, i = (2 * j) // group, (2 * j) % group
        rows_a = slice(i * tq, (i + 1) * tq)
        rows_c = slice((i + 1) * tq, (i + 2) * tq)
        a = acc_ref[g, rows_a, :] / l_ref[g, rows_a, :]
        c = acc_ref[g, rows_c, :] / l_ref[g, rows_c, :]
        if g == 0:
            c = pltpu.roll(c, HEAD_DIM, 1)
        else:
            a = pltpu.roll(a, HEAD_DIM, 1)
        o_ref[0, :, j * LANES:(j + 1) * LANES] = jnp.where(lane < HEAD_DIM, a, c).astype(BF16)


def _diff_kernel(q_ref, kt_ref, v_ref, lam_ref, sub_ref, o_ref, qs_ref, m_ref, l_ref, acc_ref, s_ref,
                 *, lam_init):
    tq = q_ref.shape[1]
    for g in range(2):
        for i in range(2):
            c0 = g * LANES + i * HEAD_DIM
            qs_ref[g, i * tq:(i + 1) * tq, :] = q_ref[0, :, c0:c0 + HEAD_DIM]
    _attn_sweep([[g * LANES, g * LANES + HEAD_DIM] for g in range(2)], [0, LANES],
                qs_ref, kt_ref, v_ref, s_ref, m_ref, l_ref, acc_ref)
    lf = lam_ref[...]
    lam = (jnp.exp(jnp.sum(lf[0:1] * lf[1:2], axis=-1, keepdims=True))
           - jnp.exp(jnp.sum(lf[2:3] * lf[3:4], axis=-1, keepdims=True)) + lam_init)
    for g in range(2):
        o = (acc_ref[g, 0:tq, :] / l_ref[g, 0:tq, :]
             - lam * (acc_ref[g, tq:2 * tq, :] / l_ref[g, tq:2 * tq, :]))
        ms = jnp.mean(o * o, axis=-1, keepdims=True)
        o = (o * lax.rsqrt(ms + EPS)) * sub_ref[...] * (1.0 - lam_init)
        o_ref[0, :, g * LANES:(g + 1) * LANES] = o.astype(BF16)


def _attn_scratch(rows, tk):
    return [pltpu.VMEM((2, rows, HEAD_DIM), BF16), pltpu.VMEM((2, rows, 1), F32),
            pltpu.VMEM((2, rows, 1), F32), pltpu.VMEM((2, rows, LANES), F32),
            pltpu.VMEM((2, rows, tk), F32)]


def _key_chunk(n_keys, rows):
    cap = min(SCORE_ELEMS // rows, n_keys)
    return max(c for c in range(LANES, cap + 1, LANES) if n_keys % c == 0)


_KV_A_BLOCK = B_HEADS * 2 * HEAD_DIM // LANES


def _gqa(q, kt, v):
    b, lq, _ = q.shape
    tq = min(Q_TILE, lq)
    n_keys = kt.shape[2]
    blk = _KV_A_BLOCK
    return pl.pallas_call(
        _gqa_kernel,
        grid=(b, lq // tq),
        in_specs=[pl.BlockSpec((1, tq, A_HEADS * HEAD_DIM), lambda i, j: (i, j, 0)),
                  pl.BlockSpec((1, LANES, n_keys), lambda i, j: (i, blk, 0)),
                  pl.BlockSpec((1, n_keys, LANES), lambda i, j: (i, 0, blk))],
        out_specs=pl.BlockSpec((1, tq, A_HEADS * HEAD_DIM), lambda i, j: (i, j, 0)),
        out_shape=jax.ShapeDtypeStruct((b, lq, A_HEADS * HEAD_DIM), BF16),
        scratch_shapes=_attn_scratch(A_HEADS // A_KV_HEADS * tq,
                                     _key_chunk(n_keys, A_HEADS // A_KV_HEADS * tq)),
        compiler_params=_params("parallel", "parallel"),
        name="gqa_attn",
    )(q, kt, v)


def _diff(q, kt, v, b_lambda, b_subln, lam_init):
    b, lq, _ = q.shape
    tq = min(Q_TILE, lq)
    n_keys = kt.shape[2]
    w = 2 * LANES
    q0 = A_HEADS * HEAD_DIM // w
    return pl.pallas_call(
        functools.partial(_diff_kernel, lam_init=lam_init),
        grid=(b, B_HEADS // 2, lq // tq),
        in_specs=[pl.BlockSpec((1, tq, w), lambda i, h, j: (i, j, q0 + h)),
                  pl.BlockSpec((1, w, n_keys), lambda i, h, j: (i, h, 0)),
                  pl.BlockSpec((1, n_keys, w), lambda i, h, j: (i, 0, h)),
                  pl.BlockSpec((4, HEAD_DIM), lambda i, h, j: (0, 0)),
                  pl.BlockSpec((1, LANES), lambda i, h, j: (0, 0))],
        out_specs=pl.BlockSpec((1, tq, w), lambda i, h, j: (i, j, h)),
        out_shape=jax.ShapeDtypeStruct((b, lq, B_HEADS * LANES), BF16),
        scratch_shapes=_attn_scratch(2 * tq, _key_chunk(n_keys, 2 * tq)),
        compiler_params=_params("parallel", "parallel", "parallel"),
        name="diff_attn",
    )(q, kt, v, b_lambda, b_subln)


def _na_kernel(q_ref, km_ref, k0_ref, kp_ref, kc_ref, vm_ref, v0_ref, vp_ref, vc_ref, bias_ref,
               o_ref, *, rows):
    i = pl.program_id(1)
    tq = q_ref.shape[1]
    nb = 3 * tq
    qrow = NA_BLOCK_ROWS * i + lax.broadcasted_iota(jnp.int32, (tq, nb), 0) // GRID_W
    krow = NA_BLOCK_ROWS * (i - 1) + lax.broadcasted_iota(jnp.int32, (tq, nb), 1) // GRID_W
    rs = jnp.clip(qrow - NA_ROWS // 2, 0, rows - NA_ROWS)
    row_mask = jnp.where((krow >= rs) & (krow < rs + NA_ROWS), 0.0, NEG).astype(F32)
    lane = lax.broadcasted_iota(jnp.int32, (tq, LANES), 1)
    kts = (km_ref, k0_ref, kp_ref)
    vs = (vm_ref, v0_ref, vp_ref)
    for pair in range(C_HEADS // 2):
        outs = []
        for h in (2 * pair, 2 * pair + 1):
            r0 = h * HEAD_DIM
            q = q_ref[0, :, r0:r0 + HEAD_DIM]
            s_band = jnp.concatenate(
                [jnp.dot(q, k[0, r0:r0 + HEAD_DIM, :], preferred_element_type=F32) for k in kts],
                axis=1) + bias_ref[h] + row_mask
            s_ctx = jnp.dot(q, kc_ref[0, r0:r0 + HEAD_DIM, :], preferred_element_type=F32)
            m = jnp.maximum(jnp.max(s_band, axis=-1, keepdims=True),
                            jnp.max(s_ctx, axis=-1, keepdims=True))
            p_band = jnp.exp2(s_band - m)
            p_ctx = jnp.exp2(s_ctx - m)
            denom = jnp.sum(p_band, axis=-1, keepdims=True) + jnp.sum(p_ctx, axis=-1, keepdims=True)
            c0 = pair * LANES
            pv = jnp.dot(p_ctx.astype(BF16), vc_ref[0, :, c0:c0 + LANES], preferred_element_type=F32)
            for t, v in enumerate(vs):
                pv += jnp.dot(p_band[:, t * tq:(t + 1) * tq].astype(BF16), v[0, :, c0:c0 + LANES],
                              preferred_element_type=F32)
            outs.append(pv / denom)
        a, c = outs
        o_ref[0, :, pair * LANES:(pair + 1) * LANES] = jnp.where(lane < HEAD_DIM, a, c).astype(BF16)


def _na(q, kt_lat, v_lat, kt_ctx, v_ctx, bias):
    b, s, w = q.shape
    tq = NA_BLOCK_ROWS * GRID_W
    nblk = s // tq
    nc = kt_ctx.shape[2]
    rows = s // GRID_W

    def kt_spec(d):
        return pl.BlockSpec((1, w, tq), lambda i, j: (i, 0, jnp.clip(j + d, 0, nblk - 1)))

    def v_spec(d):
        return pl.BlockSpec((1, tq, w), lambda i, j: (i, jnp.clip(j + d, 0, nblk - 1), 0))

    return pl.pallas_call(
        functools.partial(_na_kernel, rows=rows),
        grid=(b, nblk),
        in_specs=[pl.BlockSpec((1, tq, w), lambda i, j: (i, j, 0)),
                  kt_spec(-1), kt_spec(0), kt_spec(1),
                  pl.BlockSpec((1, w, nc), lambda i, j: (i, 0, 0)),
                  v_spec(-1), v_spec(0), v_spec(1),
                  pl.BlockSpec((1, nc, w), lambda i, j: (i, 0, 0)),
                  pl.BlockSpec(bias.shape, lambda i, j: (0, 0, 0))],
        out_specs=pl.BlockSpec((1, tq, w), lambda i, j: (i, j, 0)),
        out_shape=jax.ShapeDtypeStruct((b, s, w), BF16),
        compiler_params=_params("parallel", "parallel"),
        name="na_attn",
    )(q, kt_lat, kt_lat, kt_lat, kt_ctx, v_lat, v_lat, v_lat, v_ctx, bias)


def _na_bias_table(rpb):
    heads = rpb.shape[0]
    nr, nc = 2 * NA_ROWS - 1, 2 * NA_COLS - 1
    j = np.arange(GRID_W)
    dc = j[None, :] - j[:, None] + (NA_COLS - 1)
    cs = np.clip(j - NA_COLS // 2, 0, GRID_W - NA_COLS)
    col_ok = (j[None, :] >= cs[:, None]) & (j[None, :] < cs[:, None] + NA_COLS)
    onehot = (np.arange(nc)[:, None, None] == dc[None]) & col_ok[None]
    cols = jnp.einsum('hrm,mjk->hrjk', rpb, jnp.asarray(onehot, F32),
                      precision=lax.Precision.HIGHEST)
    cols = jnp.where(col_ok[None, None], cols, NEG)
    band = 3 * NA_BLOCK_ROWS
    per_q = [cols[:, NA_ROWS - 1 - NA_BLOCK_ROWS - qr:NA_ROWS - 1 - NA_BLOCK_ROWS - qr + band]
             for qr in range(NA_BLOCK_ROWS)]
    t = jnp.stack(per_q, axis=1)
    assert t.shape == (heads, NA_BLOCK_ROWS, band, GRID_W, GRID_W) and nr == band + NA_BLOCK_ROWS - 1
    return t.transpose(0, 1, 3, 2, 4).reshape(heads, NA_BLOCK_ROWS * GRID_W, band * GRID_W)


def _out_kernel(o1_ref, o2_ref, w_ref, x_ref, mod_ref, g_ref, x1_ref, ht_ref):
    half = o1_ref.shape[2]
    o = (jnp.dot(o1_ref[0], w_ref[0:half, :], preferred_element_type=F32)
         + jnp.dot(o2_ref[0], w_ref[half:, :], preferred_element_type=F32))
    x1 = x_ref[0] + mod_ref[0, 2:3, :] * o
    x1_ref[0] = x1
    h2 = _modulated_norm(x1, g_ref[...], mod_ref[0, 3:4, :], mod_ref[0, 4:5, :])
    ht_ref[0] = pltpu.bitcast(h2.T.astype(BF16), jnp.int32)


def _out(o1, o1_blk, o2, o2_blk, w, x, mod, gain):
    b, l, d = x.shape
    tm = min(ROW_TILE, l)
    half = d // 2
    mod_map = (lambda i, j: (i, 0, 0)) if mod.shape[0] > 1 else (lambda i, j: (0, 0, 0))
    return pl.pallas_call(
        _out_kernel,
        grid=(b, l // tm),
        in_specs=[pl.BlockSpec((1, tm, half), lambda i, j: (i, j, o1_blk)),
                  pl.BlockSpec((1, tm, half), lambda i, j: (i, j, o2_blk)),
                  pl.BlockSpec((d, d), lambda i, j: (0, 0)),
                  pl.BlockSpec((1, tm, d), lambda i, j: (i, j, 0)),
                  pl.BlockSpec((1, 6, d), mod_map),
                  pl.BlockSpec((1, d), lambda i, j: (0, 0))],
        out_specs=[pl.BlockSpec((1, tm, d), lambda i, j: (i, j, 0)),
                   pl.BlockSpec((1, d // 2, tm), lambda i, j: (i, 0, j))],
        out_shape=[jax.ShapeDtypeStruct((b, l, d), F32),
                   jax.ShapeDtypeStruct((b, d // 2, l), jnp.int32)],
        compiler_params=_params("parallel", "parallel"),
        name="out_proj",
    )(o1, o2, w, x, mod, gain)


def _top16(s, iota, exact):
    rank = jnp.full(s.shape, float(PEER_TOPK), F32)
    big = float(s.shape[0])
    vals = []
    for r in range(PEER_TOPK):
        m = jnp.max(s, axis=0, keepdims=True)
        hit = s == m
        if exact:
            hit = iota == jnp.min(jnp.where(hit, iota, big), axis=0, keepdims=True)
        rank = jnp.where(hit, float(r), rank)
        s = jnp.where(hit, -jnp.inf, s)
        vals.append(m)
    return vals, rank


def _rows_tile(rows, iota8):
    t = jnp.zeros((SUBLANES,) + rows[0].shape[1:], F32)
    for k, r in enumerate(rows):
        t = jnp.where(iota8 == k, r, t)
    return t


_CAND_COUNTS = [PEER_TOPK // (a + 1) for a in range(SUBLANES)]


def _retrieve(s1, s2, exact):
    iota = lax.broadcasted_iota(jnp.int32, (PEER_N_KEYS, LANES), 0).astype(F32)
    iota8 = lax.broadcasted_iota(jnp.int32, (SUBLANES, LANES), 0)
    iota_c = lax.broadcasted_iota(jnp.int32, (SUBLANES * (SUBLANES + 2), LANES), 0).astype(F32)
    v1, r1 = _top16(s1, iota, exact)
    v2, r2 = _top16(s2, iota, exact)
    e1 = [jnp.exp(v - v1[0]) for v in v1]
    e2 = [jnp.exp(v - v2[0]) for v in v2]
    v2_lo, v2_hi = _rows_tile(v2[:8], iota8), _rows_tile(v2[8:], iota8)
    e2_lo, e2_hi = _rows_tile(e2[:8], iota8), _rows_tile(e2[8:], iota8)
    v1_hi, e1_hi = _rows_tile(v1[8:], iota8), _rows_tile(e1[8:], iota8)
    cand, ecand = [], []
    for a, cnt in enumerate(_CAND_COUNTS):
        cand.append(jnp.where(iota8 < cnt, v1[a] + v2_lo, -jnp.inf))
        ecand.append(e1[a] * e2_lo)
        if a == 0:
            cand.append(v1[0] + v2_hi)
            ecand.append(e1[0] * e2_hi)
    cand.append(v1_hi + v2[0])
    ecand.append(e1_hi * e2[0])
    cand = jnp.concatenate(cand, axis=0)
    ecand = jnp.concatenate(ecand, axis=0)
    _, rc = _top16(cand, iota_c, exact)
    sel = (rc < float(PEER_TOPK)).astype(F32)
    ranked = (jnp.sum(sel, axis=0, keepdims=True)
              + jnp.sum((r1 < float(PEER_TOPK)).astype(F32), axis=0, keepdims=True)
              + jnp.sum((r2 < float(PEER_TOPK)).astype(F32), axis=0, keepdims=True))
    z = jnp.sum(sel * ecand, axis=0, keepdims=True)
    n_rows = [jnp.sum(sel[0:2 * SUBLANES], axis=0, keepdims=True)]
    for a in range(1, SUBLANES):
        n_rows.append(jnp.sum(sel[SUBLANES * (a + 1):SUBLANES * (a + 2)], axis=0, keepdims=True))
    n_hi = sel[SUBLANES * (SUBLANES + 1):]
    n1 = jnp.zeros(s1.shape, F32)
    for a in range(PEER_TOPK):
        na = n_rows[a] if a < SUBLANES else n_hi[a - SUBLANES:a - SUBLANES + 1]
        n1 = jnp.where(r1 == float(a), na, n1)
    return n1, jnp.exp(s1 - v1[0]) / z, r2, jnp.exp(s2 - v2[0]), ranked


def _topk_kernel(ht_ref, wq_ref, keys_ref, n1_ref, c1_ref, b2_ref, e2_ref, qt_ref, s_ref):
    chunks = ht_ref.shape[2] // LANES
    qt_ref[...] = jnp.dot(wq_ref[...], pltpu.bitcast(ht_ref[0], BF16),
                          preferred_element_type=F32).astype(BF16)
    for hp in range(2 * PEER_HEADS):
        s_ref[hp] = jnp.dot(keys_ref[hp], qt_ref[hp * PEER_N_KEYS:(hp + 1) * PEER_N_KEYS, :],
                            preferred_element_type=F32)

    def emit(h, sl, exact):
        n1, c1, b2, e2, ranked = _retrieve(s_ref[2 * h, :, sl], s_ref[2 * h + 1, :, sl], exact)
        n1_ref[0, h, :, sl] = n1
        c1_ref[0, h, :, sl] = c1
        b2_ref[0, h, :, sl] = pltpu.bitcast(b2.astype(BF16), jnp.int32)
        e2_ref[0, h, :, sl] = pltpu.bitcast(e2.astype(BF16), jnp.int32)
        return ranked

    def pair(i, carry):
        h = i // (chunks // 2)
        c0 = (i % (chunks // 2)) * 2 * LANES
        sls = [pl.ds(pl.multiple_of(c0 + j * LANES, LANES), LANES) for j in range(2)]
        ranked = jnp.maximum(emit(h, sls[0], False), emit(h, sls[1], False))

        @pl.when(jnp.max(ranked) > float(3 * PEER_TOPK))
        def _():
            for sl in sls:
                emit(h, sl, True)

        return carry

    lax.fori_loop(0, PEER_HEADS * chunks // 2, pair, 0)


def _peer_topk(ht, wq_t, keys, tt):
    b, d2, l = ht.shape
    nqd = wq_t.shape[0]
    shp = (b, PEER_HEADS, PEER_N_KEYS, l)
    shp_packed = (b, PEER_HEADS, PEER_N_KEYS // 2, l)
    ospec = pl.BlockSpec((1, PEER_HEADS, PEER_N_KEYS, tt), lambda i, j: (i, 0, 0, j))
    ospec_packed = pl.BlockSpec((1, PEER_HEADS, PEER_N_KEYS // 2, tt), lambda i, j: (i, 0, 0, j))
    return pl.pallas_call(
        _topk_kernel,
        grid=(b, l // tt),
        in_specs=[pl.BlockSpec((1, d2, tt), lambda i, j: (i, 0, j)),
                  pl.BlockSpec((nqd, 2 * d2), lambda i, j: (0, 0)),
                  pl.BlockSpec(keys.shape, lambda i, j: (0, 0, 0))],
        out_specs=[ospec, ospec, ospec_packed, ospec_packed],
        out_shape=[jax.ShapeDtypeStruct(shp, F32), jax.ShapeDtypeStruct(shp, F32),
                   jax.ShapeDtypeStruct(shp_packed, jnp.int32),
                   jax.ShapeDtypeStruct(shp_packed, jnp.int32)],
        scratch_shapes=[pltpu.VMEM((nqd, tt), BF16),
                        pltpu.VMEM((2 * PEER_HEADS, PEER_N_KEYS, tt), F32)],
        compiler_params=_params("parallel", "parallel"),
        name="peer_topk",
    )(ht, wq_t, keys)


def _peer_kernel(ht_ref, n1_ref, c1_ref, b2_ref, e2_ref, u_ref, vt_ref, x_ref, mod_ref, o_ref,
                 a_ref, g_ref, acc_ref):
    e = pl.program_id(2)
    et = 2 * u_ref.shape[0]
    ht = pltpu.bitcast(ht_ref[0], BF16)

    @pl.when(e == 0)
    def _():
        acc_ref[...] = jnp.zeros(acc_ref.shape, F32)

    a_ref[...] = jnp.dot(pltpu.bitcast(u_ref[...], BF16), ht, preferred_element_type=F32)
    tile3 = (PEER_N_KEYS // BF16_ROWS, BF16_ROWS, LANES)
    for k in range(et // PEER_N_KEYS):
        rows = slice(k * PEER_N_KEYS, (k + 1) * PEER_N_KEYS)
        for c in range(ht.shape[1] // LANES):
            sl = slice(c * LANES, (c + 1) * LANES)
            w = jnp.zeros(tile3, BF16)
            for h in range(PEER_HEADS):
                n_t = jnp.broadcast_to(n1_ref[0, h, k:k + 1, sl], tile3[1:]).astype(BF16)[None]
                c_t = jnp.broadcast_to(c1_ref[0, h, k:k + 1, sl], tile3[1:]).astype(BF16)[None]
                b2 = pltpu.bitcast(b2_ref[0, h, :, sl], BF16).reshape(tile3)
                e2 = pltpu.bitcast(e2_ref[0, h, :, sl], BF16).reshape(tile3)
                w = w + jnp.clip(n_t - b2, 0, 1) * (e2 * c_t)
            a = a_ref[rows, sl].astype(BF16).reshape(tile3)
            z = a * (a * a * (-2.0 * _GELU_K * 0.044715 * _LOG2E) - 2.0 * _GELU_K * _LOG2E)
            act = a / (1.0 + jnp.exp2(z))
            g_ref[rows, sl] = (w * act).reshape(PEER_N_KEYS, LANES)
    acc_ref[...] += jnp.dot(pltpu.bitcast(vt_ref[...], BF16), g_ref[...],
                            preferred_element_type=F32)

    @pl.when(e == pl.num_programs(2) - 1)
    def _():
        o_ref[0] = x_ref[0] + mod_ref[0, 5:6, :] * acc_ref[...].T


def _peer_dense(ht, n1, c1, b2, e2, u, vt, x, mod, tt):
    b, d2, l = ht.shape
    d = 2 * d2
    ne = 2 * u.shape[0]
    et = PEER_EXPERTS
    mod_map = (lambda i, j, k: (i, 0, 0)) if mod.shape[0] > 1 else (lambda i, j, k: (0, 0, 0))
    sel_spec = pl.BlockSpec((1, PEER_HEADS, PEER_N_KEYS // 2, tt), lambda i, j, k: (i, 0, 0, j))
    row_spec = pl.BlockSpec((1, PEER_HEADS, et // PEER_N_KEYS, tt), lambda i, j, k: (i, 0, k, j))
    return pl.pallas_call(
        _peer_kernel,
        grid=(b, l // tt, ne // et),
        in_specs=[pl.BlockSpec((1, d2, tt), lambda i, j, k: (i, 0, j)),
                  row_spec, row_spec, sel_spec, sel_spec,
                  pl.BlockSpec((et // 2, d), lambda i, j, k: (k, 0)),
                  pl.BlockSpec((d2, et), lambda i, j, k: (0, k)),
                  pl.BlockSpec((1, tt, d), lambda i, j, k: (i, j, 0)),
                  pl.BlockSpec((1, 6, d), mod_map)],
        out_specs=pl.BlockSpec((1, tt, d), lambda i, j, k: (i, j, 0)),
        out_shape=jax.ShapeDtypeStruct((b, l, d), F32),
        scratch_shapes=[pltpu.VMEM((et, tt), F32), pltpu.VMEM((et, tt), BF16),
                        pltpu.VMEM((d, tt), F32)],
        compiler_params=_params("parallel", "parallel", "arbitrary"),
        name="peer_mix",
    )(ht, n1, c1, b2, e2, u, vt, x, mod)


def _peer(ht, x1, mod, wq_t, keys, u, vt):
    tt = min(PEER_TOKENS, ht.shape[2])
    n1, c1, b2, e2 = _peer_topk(ht, wq_t, keys, tt)
    return _peer_dense(ht, n1, c1, b2, e2, u, vt, x1, mod, tt)


def _rope_tables(s):
    t = np.arange(s)
    pos = np.stack([t // GRID_W, t % GRID_W], axis=1).astype(np.float32)
    axis_dim = HEAD_DIM // 2
    freqs = 1.0 / (ROPE_THETA ** (jnp.arange(0, axis_dim, 2, dtype=F32) / axis_dim))
    ang = jnp.asarray(pos)[:, :, None] * freqs[None, None, :]
    cos, sin = jnp.cos(ang), jnp.sin(ang)
    zero = jnp.zeros_like(sin)
    def lanes(first_half, second_half):
        p = jnp.stack([first_half, second_half], axis=2).reshape(s, HEAD_DIM)
        return jnp.concatenate([p, p], axis=1)
    return lanes(cos, cos), lanes(-sin, zero), lanes(zero, sin)


def _prep_kernel(u_ref, v_ref, up_ref, vtp_ref):
    up_ref[...] = pltpu.bitcast(u_ref[0].astype(BF16), jnp.int32)
    vtp_ref[...] = pltpu.bitcast(v_ref[0].T.astype(BF16), jnp.int32)


def _prep_experts(peer_u, peer_v, layer):
    _, ne, d = peer_u.shape
    te = PEER_EXPERTS // 2
    return pl.pallas_call(
        _prep_kernel,
        grid=(ne // te,),
        in_specs=[pl.BlockSpec((1, te, d), lambda i: (layer, i, 0)),
                  pl.BlockSpec((1, te, d), lambda i: (layer, i, 0))],
        out_specs=[pl.BlockSpec((te // 2, d), lambda i: (i, 0)),
                   pl.BlockSpec((d // 2, te), lambda i: (0, i))],
        out_shape=[jax.ShapeDtypeStruct((ne // 2, d), jnp.int32),
                   jax.ShapeDtypeStruct((d // 2, ne), jnp.int32)],
        compiler_params=_params("parallel"),
        name="prep_experts",
    )(peer_u, peer_v)


def _block_diag_mean():
    i = np.arange(LANES)
    return jnp.asarray((i[:, None] // HEAD_DIM == i[None, :] // HEAD_DIM) / HEAD_DIM, BF16)


def kernel(x, c, ctx, c_ctx, ada_w, ada_b, norm_g, ab_w_in, ab_w_out, a_q_norm, a_k_norm, b_q_norm,
           b_k_norm, b_lambda, b_subln, c_w_in, c_w_out, c_q_norm, c_k_norm, c_rpb, peer_w_q,
           peer_keys, peer_u, peer_v):
    bsz, s, d = x.shape
    depth = ada_w.shape[0]
    assert depth == 2, "one attention-pair layer followed by one (final) neighbourhood layer"
    log2e = math.log2(math.e)
    scale = HEAD_DIM ** -0.5 * log2e

    cs = jnp.zeros((SUBLANES, d), F32).at[:bsz].set(c).at[bsz].set(c_ctx)
    mods = _ada(cs, ada_w, ada_b)
    bd = _block_diag_mean()
    rope = _rope_tables(s)

    for l in range(depth):
        last = l == depth - 1
        mod_lat = mods[l, :bsz].reshape(bsz, 6, d)
        mod_ctx = mods[l, bsz:bsz + 1].reshape(1, 6, d)
        g1 = norm_g[l, 0].reshape(1, d)
        g2 = norm_g[l, 1].reshape(1, d)
        i = l // 2
        wq_t = peer_w_q[l].T.astype(BF16)
        keys = peer_keys[l].reshape(2 * PEER_HEADS, PEER_N_KEYS, -1).astype(BF16)
        u, vt = _prep_experts(peer_u, peer_v, l)
        if l % 2 == 0:
            lam_init = 0.8 - 0.6 * math.exp(-0.3 * l)
            qa, ka, va, qb, kb, vb = jnp.split(
                ab_w_in[i], np.cumsum([512, 128, 128, 512, 512, 512])[:-1].tolist(), axis=1)
            w = jnp.concatenate([qa, qb, kb, ka, vb, va], axis=1).astype(BF16)
            nq, nk, nv = 1024, 640, 640
            gq = jnp.concatenate([jnp.tile(a_q_norm[i], A_HEADS), jnp.tile(b_q_norm[i], 2 * B_HEADS)])
            gk = jnp.concatenate([jnp.tile(b_k_norm[i], 2 * B_HEADS), jnp.tile(a_k_norm[i], A_KV_HEADS)])
            gq = (gq * scale).reshape(1, nq)
            gk = gk.reshape(1, nk)
            q_l, kt_l, v_l = _proj(x, mod_lat, g1, w, gq, gk, bd, rope, nq, nk, nv)
            q_c, kt_c, v_c = _proj(ctx, mod_ctx, g1, w, gq, gk, bd, None, nq, nk, nv)
            lam2 = b_lambda[i]
            sub = b_subln[i].reshape(1, LANES)
            kt_all = jnp.concatenate([kt_l, kt_c], axis=2)
            v_all = jnp.concatenate([v_l, v_c], axis=1)
            oa = _gqa(q_l, kt_all, v_all)
            ob = _diff(q_l, kt_all, v_all, lam2, sub, lam_init)
            w_out = ab_w_out[i].astype(BF16)
            x1, ht = _out(oa, 0, ob, 0, w_out, x, mod_lat, g2)
            if not last:
                oa_c = _gqa(q_c, kt_c, v_c)
                ob_c = _diff(q_c, kt_c, v_c, lam2, sub, lam_init)
                ctx1, ht_c = _out(oa_c, 0, ob_c, 0, w_out, ctx, mod_ctx, g2)
        else:
            w = c_w_in[i].astype(BF16)
            nq = nk = nv = C_HEADS * HEAD_DIM
            gq = (jnp.tile(c_q_norm[i], C_HEADS) * scale).reshape(1, nq)
            gk = jnp.tile(c_k_norm[i], C_HEADS).reshape(1, nk)
            q_l, kt_l, v_l = _proj(x, mod_lat, g1, w, gq, gk, bd, None, nq, nk, nv)
            q_c, kt_c, v_c = _proj(ctx, mod_ctx, g1, w, gq, gk, bd, None, nq, nk, nv)
            o = _na(q_l, kt_l, v_l, kt_c, v_c, _na_bias_table(c_rpb[i] * log2e))
            w_out = c_w_out[i].astype(BF16)
            x1, ht = _out(o, 0, o, 1, w_out, x, mod_lat, g2)
        x = _peer(ht, x1, mod_lat, wq_t, keys, u, vt)
        if not last:
            ctx = _peer(ht_c, ctx1, mod_ctx, wq_t, keys, u, vt)
    return x
```

```python
import functools
import math

import numpy as np
import jax
import jax.numpy as jnp
from jax import lax
from jax.experimental import pallas as pl
from jax.experimental.pallas import tpu as pltpu

F32 = jnp.float32
BF16 = jnp.bfloat16

GRID_W = 64
HEAD_DIM = 64
ROPE_THETA = 10000.0
EPS = 1e-6
A_HEADS = 8
A_KV_HEADS = 2
B_HEADS = 4
C_HEADS = 16
NA_ROWS = 8
NA_COLS = 16
PEER_HEADS = 8
PEER_N_KEYS = 128
PEER_TOPK = 16

LANES = 128
SUBLANES = 8
BF16_ROWS = 16
VMEM_LIMIT = 56 * 1024 * 1024
NEG = -1e30

ADA_COLS = 1536
ROW_TILE = 512
Q_TILE = 256
GQA_SCORE_ELEMS = 3072 * 1024
DIFF_SCORE_ELEMS = 1536 * 1024
NA_BLOCK_ROWS = 4
PEER_TOKENS = 512
PEER_MIX_TOKENS = 1024
PEER_EXPERTS = 1024
PEER_GATE_LANES = LANES
_GELU_K = math.sqrt(2.0 / math.pi)
_LOG2E = math.log2(math.e)


def _params(*sem):
    return pltpu.CompilerParams(dimension_semantics=sem, vmem_limit_bytes=VMEM_LIMIT)


def _ada_kernel(cs_ref, w_ref, b_ref, o_ref):
    cs = cs_ref[...]
    a = cs * jax.nn.sigmoid(cs)
    o_ref[0] = jnp.dot(a, w_ref[0], preferred_element_type=F32,
                       precision=lax.Precision.HIGHEST) + b_ref[0]


def _ada(cs, ada_w, ada_b):
    depth, d, n = ada_w.shape
    tn = ADA_COLS
    return pl.pallas_call(
        _ada_kernel,
        grid=(depth, n // tn),
        in_specs=[pl.BlockSpec((SUBLANES, d), lambda l, j: (0, 0)),
                  pl.BlockSpec((1, d, tn), lambda l, j: (l, 0, j)),
                  pl.BlockSpec((1, 1, tn), lambda l, j: (l, 0, j))],
        out_specs=pl.BlockSpec((1, SUBLANES, tn), lambda l, j: (l, 0, j)),
        out_shape=jax.ShapeDtypeStruct((depth, SUBLANES, n), F32),
        compiler_params=_params("parallel", "parallel"),
        name="ada_mod",
    )(cs, ada_w, ada_b.reshape(depth, 1, n))


def _modulated_norm(x, gain, shift, scale):
    ms = jnp.mean(x * x, axis=-1, keepdims=True)
    return (x * lax.rsqrt(ms + EPS)) * gain * (1.0 + scale) + shift


def _proj_kernel(*refs, nq, nk, rope):
    if rope:
        (x_ref, mod_ref, g_ref, w_ref, gq_ref, gk_ref, bd_ref, cos_ref, sa_ref, sb_ref,
         q_ref, kt_ref, v_ref) = refs
    else:
        x_ref, mod_ref, g_ref, w_ref, gq_ref, gk_ref, bd_ref, q_ref, kt_ref, v_ref = refs
    h = _modulated_norm(x_ref[0], g_ref[...], mod_ref[0, 0:1, :], mod_ref[0, 1:2, :])
    p = jnp.dot(h.astype(BF16), w_ref[...], preferred_element_type=F32)
    bd = bd_ref[...]

    def head_norm(c0, gain):
        yc = p[:, c0:c0 + LANES]
        ss = jnp.dot((yc * yc).astype(BF16), bd, preferred_element_type=F32)
        yn = yc * lax.rsqrt(ss + EPS) * gain
        if rope:
            yn = (yn * cos_ref[...] + pltpu.roll(yn, LANES - 16, 1) * sa_ref[...]
                  + pltpu.roll(yn, 16, 1) * sb_ref[...])
        return yn

    for c in range(nq // LANES):
        c0 = c * LANES
        q_ref[0, :, c0:c0 + LANES] = head_norm(c0, gq_ref[:, c0:c0 + LANES]).astype(BF16)
    for c in range(nk // LANES):
        c0 = c * LANES
        yn = head_norm(nq + c0, gk_ref[:, c0:c0 + LANES])
        kt_ref[0, c0:c0 + LANES, :] = yn.T.astype(BF16)
    ones = jnp.ones((p.shape[0], LANES), BF16)
    for c in range((p.shape[1] - nq - nk) // LANES):
        c0 = nq + nk + c * LANES
        v_ref[0, :, 2 * c * LANES:(2 * c + 1) * LANES] = p[:, c0:c0 + LANES].astype(BF16)
        v_ref[0, :, (2 * c + 1) * LANES:(2 * c + 2) * LANES] = ones


def _proj(x, mod, gain, w, gq, gk, bd, rope_tabs, nq, nk, nv):
    b, l, d = x.shape
    tm = min(ROW_TILE, l)
    n = nq + nk + nv
    mod_map = (lambda i, j: (i, 0, 0)) if mod.shape[0] > 1 else (lambda i, j: (0, 0, 0))
    in_specs = [pl.BlockSpec((1, tm, d), lambda i, j: (i, j, 0)),
                pl.BlockSpec((1, 6, d), mod_map),
                pl.BlockSpec((1, d), lambda i, j: (0, 0)),
                pl.BlockSpec((d, n), lambda i, j: (0, 0)),
                pl.BlockSpec((1, nq), lambda i, j: (0, 0)),
                pl.BlockSpec((1, nk), lambda i, j: (0, 0)),
                pl.BlockSpec((LANES, LANES), lambda i, j: (0, 0))]
    args = [x, mod, gain, w, gq, gk, bd]
    if rope_tabs is not None:
        in_specs += [pl.BlockSpec((tm, LANES), lambda i, j: (j, 0))] * 3
        args += list(rope_tabs)
    return pl.pallas_call(
        functools.partial(_proj_kernel, nq=nq, nk=nk, rope=rope_tabs is not None),
        grid=(b, l // tm),
        in_specs=in_specs,
        out_specs=[pl.BlockSpec((1, tm, nq), lambda i, j: (i, j, 0)),
                   pl.BlockSpec((1, nk, tm), lambda i, j: (i, 0, j)),
                   pl.BlockSpec((1, tm, 2 * nv), lambda i, j: (i, j, 0))],
        out_shape=[jax.ShapeDtypeStruct((b, l, nq), BF16),
                   jax.ShapeDtypeStruct((b, nk, l), BF16),
                   jax.ShapeDtypeStruct((b, l, 2 * nv), BF16)],
        compiler_params=_params("parallel", "parallel"),
        name="in_proj",
    )(*args)


def _online_update(g, s, v, m_ref, acc_ref):
    m_prev = m_ref[g]
    m_new = jnp.maximum(m_prev, jnp.max(s, axis=-1, keepdims=True))
    alpha = jnp.exp2(m_prev - m_new)
    p = jnp.exp2(s - m_new)
    acc_ref[g] = alpha * acc_ref[g] + jnp.dot(p.astype(BF16), v, preferred_element_type=F32)
    m_ref[g] = m_new


def _normalised(acc_ref, g, rows):
    return acc_ref[g, rows, :LANES] / acc_ref[g, rows, LANES:LANES + 1]


def _attn_sweep(kt_rows, v_lane, qs_ref, kt_ref, v_ref, s_ref, m_ref, acc_ref):
    tq = qs_ref.shape[1] // len(kt_rows[0])
    m_ref[...] = jnp.full(m_ref.shape, -jnp.inf, F32)
    acc_ref[...] = jnp.zeros(acc_ref.shape, F32)

    def scores(g, kt):
        rows = kt_rows[g]
        if len(set(rows)) == 1:
            return jnp.dot(qs_ref[g], kt(rows[0]), preferred_element_type=F32)
        return jnp.concatenate(
            [jnp.dot(qs_ref[g, i * tq:(i + 1) * tq, :], kt(r), preferred_element_type=F32)
             for i, r in enumerate(rows)], axis=0)

    tk = s_ref.shape[2]
    n_chunks = kt_ref.shape[2] // tk

    def kt_chunk(c):
        off = pl.multiple_of(c * tk, LANES)
        return lambda r: kt_ref[0, r:r + HEAD_DIM, pl.ds(off, tk)]

    def v_chunk(c, g):
        off = pl.multiple_of(c * tk, LANES)
        return v_ref[0, pl.ds(off, tk), v_lane[g]:v_lane[g] + 2 * LANES]

    s_ref[0] = scores(0, kt_chunk(0))

    def body(c, carry):
        s_ref[1] = scores(1, kt_chunk(c))
        _online_update(0, s_ref[0], v_chunk(c, 0), m_ref, acc_ref)
        s_ref[0] = scores(0, kt_chunk(jnp.minimum(c + 1, n_chunks - 1)))
        _online_update(1, s_ref[1], v_chunk(c, 1), m_ref, acc_ref)
        return carry

    lax.fori_loop(0, n_chunks, body, 0)


def _gqa_kernel(q_ref, kt_ref, v_ref, o_ref, qs_ref, m_ref, acc_ref, s_ref):
    tq = q_ref.shape[1]
    group = A_HEADS // A_KV_HEADS
    for h in range(A_HEADS):
        qs_ref[h // group, (h % group) * tq:(h % group + 1) * tq, :] = (
            q_ref[0, :, h * HEAD_DIM:(h + 1) * HEAD_DIM])
    _attn_sweep([[g * HEAD_DIM] * group for g in range(A_KV_HEADS)], [0, 0],
                qs_ref, kt_ref, v_ref, s_ref, m_ref, acc_ref)
    lane = lax.broadcasted_iota(jnp.int32, (tq, LANES), 1)
    for j in range(A_HEADS // 2):
        g, i = (2 * j) // group, (2 * j) % group
        rows_a = slice(i * tq, (i + 1) * tq)
        rows_c = slice((i + 1) * tq, (i + 2) * tq)
        a = _normalised(acc_ref, g, rows_a)
        c = _normalised(acc_ref, g, rows_c)
        if g == 0:
            c = pltpu.roll(c, HEAD_DIM, 1)
        else:
            a = pltpu.roll(a, HEAD_DIM, 1)
        o_ref[0, :, j * LANES:(j + 1) * LANES] = jnp.where(lane < HEAD_DIM, a, c).astype(BF16)


def _diff_kernel(q_ref, kt_ref, v_ref, lam_ref, sub_ref, o_ref, qs_ref, m_ref, acc_ref, s_ref,
                 *, lam_init):
    tq = q_ref.shape[1]
    for g in range(2):
        for i in range(2):
            c0 = g * LANES + i * HEAD_DIM
            qs_ref[g, i * tq:(i + 1) * tq, :] = q_ref[0, :, c0:c0 + HEAD_DIM]
    _attn_sweep([[g * LANES, g * LANES + HEAD_DIM] for g in range(2)], [0, 2 * LANES],
                qs_ref, kt_ref, v_ref, s_ref, m_ref, acc_ref)
    lf = lam_ref[...]
    lam = (jnp.exp(jnp.sum(lf[0:1] * lf[1:2], axis=-1, keepdims=True))
           - jnp.exp(jnp.sum(lf[2:3] * lf[3:4], axis=-1, keepdims=True)) + lam_init)
    for g in range(2):
        o = (_normalised(acc_ref, g, slice(0, tq))
             - lam * _normalised(acc_ref, g, slice(tq, 2 * tq)))
        ms = jnp.mean(o * o, axis=-1, keepdims=True)
        o = (o * lax.rsqrt(ms + EPS)) * sub_ref[...] * (1.0 - lam_init)
        o_ref[0, :, g * LANES:(g + 1) * LANES] = o.astype(BF16)


def _attn_scratch(rows, tk):
    return [pltpu.VMEM((2, rows, HEAD_DIM), BF16), pltpu.VMEM((2, rows, 1), F32),
            pltpu.VMEM((2, rows, 2 * LANES), F32), pltpu.VMEM((2, rows, tk), F32)]


def _key_chunk(n_keys, rows, budget):
    cap = min(budget // rows, n_keys)
    return max(c for c in range(LANES, cap + 1, LANES) if n_keys % c == 0)


_KV_A_BLOCK = B_HEADS * 2 * HEAD_DIM // LANES


def _gqa(q, kt, v):
    b, lq, _ = q.shape
    tq = min(Q_TILE, lq)
    n_keys = kt.shape[2]
    blk = _KV_A_BLOCK
    return pl.pallas_call(
        _gqa_kernel,
        grid=(b, lq // tq),
        in_specs=[pl.BlockSpec((1, tq, A_HEADS * HEAD_DIM), lambda i, j: (i, j, 0)),
                  pl.BlockSpec((1, LANES, n_keys), lambda i, j: (i, blk, 0)),
                  pl.BlockSpec((1, n_keys, 2 * LANES), lambda i, j: (i, 0, blk))],
        out_specs=pl.BlockSpec((1, tq, A_HEADS * HEAD_DIM), lambda i, j: (i, j, 0)),
        out_shape=jax.ShapeDtypeStruct((b, lq, A_HEADS * HEAD_DIM), BF16),
        scratch_shapes=_attn_scratch(A_HEADS // A_KV_HEADS * tq,
                                     _key_chunk(n_keys, A_HEADS // A_KV_HEADS * tq, GQA_SCORE_ELEMS)),
        compiler_params=_params("parallel", "parallel"),
        name="gqa_attn",
    )(q, kt, v)


def _diff(q, kt, v, b_lambda, b_subln, lam_init):
    b, lq, _ = q.shape
    tq = min(2 * Q_TILE, lq)
    n_keys = kt.shape[2]
    w = 2 * LANES
    q0 = A_HEADS * HEAD_DIM // w
    return pl.pallas_call(
        functools.partial(_diff_kernel, lam_init=lam_init),
        grid=(b, B_HEADS // 2, lq // tq),
        in_specs=[pl.BlockSpec((1, tq, w), lambda i, h, j: (i, j, q0 + h)),
                  pl.BlockSpec((1, w, n_keys), lambda i, h, j: (i, h, 0)),
                  pl.BlockSpec((1, n_keys, 2 * w), lambda i, h, j: (i, 0, h)),
                  pl.BlockSpec((4, HEAD_DIM), lambda i, h, j: (0, 0)),
                  pl.BlockSpec((1, LANES), lambda i, h, j: (0, 0))],
        out_specs=pl.BlockSpec((1, tq, w), lambda i, h, j: (i, j, h)),
        out_shape=jax.ShapeDtypeStruct((b, lq, B_HEADS * LANES), BF16),
        scratch_shapes=_attn_scratch(2 * tq, _key_chunk(n_keys, 2 * tq, DIFF_SCORE_ELEMS)),
        compiler_params=_params("parallel", "parallel", "parallel"),
        name="diff_attn",
    )(q, kt, v, b_lambda, b_subln)


def _na_kernel(q_ref, km_ref, k0_ref, kp_ref, kc_ref, vm_ref, v0_ref, vp_ref, vc_ref, bias_ref,
               o_ref, *, rows):
    i = pl.program_id(1)
    tq = q_ref.shape[1]
    nb = 3 * tq
    qrow = NA_BLOCK_ROWS * i + lax.broadcasted_iota(jnp.int32, (tq, nb), 0) // GRID_W
    krow = NA_BLOCK_ROWS * (i - 1) + lax.broadcasted_iota(jnp.int32, (tq, nb), 1) // GRID_W
    rs = jnp.clip(qrow - NA_ROWS // 2, 0, rows - NA_ROWS)
    row_mask = jnp.where((krow >= rs) & (krow < rs + NA_ROWS), 0.0, NEG).astype(F32)
    lane = lax.broadcasted_iota(jnp.int32, (tq, LANES), 1)
    kts = (km_ref, k0_ref, kp_ref)
    vs = (vm_ref, v0_ref, vp_ref)
    for pair in range(C_HEADS // 2):
        outs = []
        for h in (2 * pair, 2 * pair + 1):
            r0 = h * HEAD_DIM
            q = q_ref[0, :, r0:r0 + HEAD_DIM]
            s_band = jnp.concatenate(
                [jnp.dot(q, k[0, r0:r0 + HEAD_DIM, :], preferred_element_type=F32) for k in kts],
                axis=1) + bias_ref[h] + row_mask
            s_ctx = jnp.dot(q, kc_ref[0, r0:r0 + HEAD_DIM, :], preferred_element_type=F32)
            m = jnp.maximum(jnp.max(s_band, axis=-1, keepdims=True),
                            jnp.max(s_ctx, axis=-1, keepdims=True))
            p_band = jnp.exp2(s_band - m)
            p_ctx = jnp.exp2(s_ctx - m)
            c0 = pair * 2 * LANES
            pv = jnp.dot(p_ctx.astype(BF16), vc_ref[0, :, c0:c0 + 2 * LANES],
                         preferred_element_type=F32)
            for t, v in enumerate(vs):
                pv += jnp.dot(p_band[:, t * tq:(t + 1) * tq].astype(BF16),
                              v[0, :, c0:c0 + 2 * LANES], preferred_element_type=F32)
            outs.append(pv[:, :LANES] / pv[:, LANES:LANES + 1])
        a, c = outs
        o_ref[0, :, pair * LANES:(pair + 1) * LANES] = jnp.where(lane < HEAD_DIM, a, c).astype(BF16)


def _na(q, kt_lat, v_lat, kt_ctx, v_ctx, bias):
    b, s, w = q.shape
    tq = NA_BLOCK_ROWS * GRID_W
    nblk = s // tq
    nc = kt_ctx.shape[2]
    rows = s // GRID_W

    def kt_spec(d):
        return pl.BlockSpec((1, w, tq), lambda i, j: (i, 0, jnp.clip(j + d, 0, nblk - 1)))

    wv = v_lat.shape[2]

    def v_spec(d):
        return pl.BlockSpec((1, tq, wv), lambda i, j: (i, jnp.clip(j + d, 0, nblk - 1), 0))

    return pl.pallas_call(
        functools.partial(_na_kernel, rows=rows),
        grid=(b, nblk),
        in_specs=[pl.BlockSpec((1, tq, w), lambda i, j: (i, j, 0)),
                  kt_spec(-1), kt_spec(0), kt_spec(1),
                  pl.BlockSpec((1, w, nc), lambda i, j: (i, 0, 0)),
                  v_spec(-1), v_spec(0), v_spec(1),
                  pl.BlockSpec((1, nc, wv), lambda i, j: (i, 0, 0)),
                  pl.BlockSpec(bias.shape, lambda i, j: (0, 0, 0))],
        out_specs=pl.BlockSpec((1, tq, w), lambda i, j: (i, j, 0)),
        out_shape=jax.ShapeDtypeStruct((b, s, w), BF16),
        compiler_params=_params("parallel", "parallel"),
        name="na_attn",
    )(q, kt_lat, kt_lat, kt_lat, kt_ctx, v_lat, v_lat, v_lat, v_ctx, bias)


def _na_bias_table(rpb):
    heads = rpb.shape[0]
    nr, nc = 2 * NA_ROWS - 1, 2 * NA_COLS - 1
    j = np.arange(GRID_W)
    dc = j[None, :] - j[:, None] + (NA_COLS - 1)
    cs = np.clip(j - NA_COLS // 2, 0, GRID_W - NA_COLS)
    col_ok = (j[None, :] >= cs[:, None]) & (j[None, :] < cs[:, None] + NA_COLS)
    onehot = (np.arange(nc)[:, None, None] == dc[None]) & col_ok[None]
    cols = jnp.einsum('hrm,mjk->hrjk', rpb, jnp.asarray(onehot, F32),
                      precision=lax.Precision.HIGHEST)
    cols = jnp.where(col_ok[None, None], cols, NEG)
    band = 3 * NA_BLOCK_ROWS
    per_q = [cols[:, NA_ROWS - 1 - NA_BLOCK_ROWS - qr:NA_ROWS - 1 - NA_BLOCK_ROWS - qr + band]
             for qr in range(NA_BLOCK_ROWS)]
    t = jnp.stack(per_q, axis=1)
    assert t.shape == (heads, NA_BLOCK_ROWS, band, GRID_W, GRID_W) and nr == band + NA_BLOCK_ROWS - 1
    return t.transpose(0, 1, 3, 2, 4).reshape(heads, NA_BLOCK_ROWS * GRID_W, band * GRID_W)


def _out_kernel(o1_ref, o2_ref, w_ref, x_ref, mod_ref, g_ref, x1_ref, ht_ref):
    half = o1_ref.shape[2]
    o = (jnp.dot(o1_ref[0], w_ref[0:half, :], preferred_element_type=F32)
         + jnp.dot(o2_ref[0], w_ref[half:, :], preferred_element_type=F32))
    x1 = x_ref[0] + mod_ref[0, 2:3, :] * o
    x1_ref[0] = x1
    h2 = _modulated_norm(x1, g_ref[...], mod_ref[0, 3:4, :], mod_ref[0, 4:5, :])
    ht_ref[0] = pltpu.bitcast(h2.T.astype(BF16), jnp.int32)


def _out(o1, o1_blk, o2, o2_blk, w, x, mod, gain):
    b, l, d = x.shape
    tm = min(ROW_TILE, l)
    half = d // 2
    mod_map = (lambda i, j: (i, 0, 0)) if mod.shape[0] > 1 else (lambda i, j: (0, 0, 0))
    return pl.pallas_call(
        _out_kernel,
        grid=(b, l // tm),
        in_specs=[pl.BlockSpec((1, tm, half), lambda i, j: (i, j, o1_blk)),
                  pl.BlockSpec((1, tm, half), lambda i, j: (i, j, o2_blk)),
                  pl.BlockSpec((d, d), lambda i, j: (0, 0)),
                  pl.BlockSpec((1, tm, d), lambda i, j: (i, j, 0)),
                  pl.BlockSpec((1, 6, d), mod_map),
                  pl.BlockSpec((1, d), lambda i, j: (0, 0))],
        out_specs=[pl.BlockSpec((1, tm, d), lambda i, j: (i, j, 0)),
                   pl.BlockSpec((1, d // 2, tm), lambda i, j: (i, 0, j))],
        out_shape=[jax.ShapeDtypeStruct((b, l, d), F32),
                   jax.ShapeDtypeStruct((b, d // 2, l), jnp.int32)],
        compiler_params=_params("parallel", "parallel"),
        name="out_proj",
    )(o1, o2, w, x, mod, gain)


def _top16(s, iota, exact):
    rank = jnp.full(s.shape, float(PEER_TOPK), F32)
    big = float(s.shape[0])
    vals = []
    for r in range(PEER_TOPK):
        m = jnp.max(s, axis=0, keepdims=True)
        hit = s == m
        if exact:
            hit = iota == jnp.min(jnp.where(hit, iota, big), axis=0, keepdims=True)
        rank = jnp.where(hit, float(r), rank)
        s = jnp.where(hit, -jnp.inf, s)
        vals.append(m)
    return vals, rank


def _sorting_network(n):
    pairs = []
    p = 1
    while p < n:
        k = p
        while k >= 1:
            for j in range(k % p, n - k, 2 * k):
                for i in range(min(k, n - j - k)):
                    if (i + j) // (2 * p) == (i + j + k) // (2 * p):
                        pairs.append((i + j, i + j + k))
            k //= 2
        p *= 2
    return pairs


def _top16_values(s):
    n_tiles = s.shape[0] // SUBLANES
    w = [s[SUBLANES * j:SUBLANES * (j + 1)] for j in range(n_tiles)]
    for i, j in _sorting_network(PEER_TOPK):
        if j < n_tiles:
            w[i], w[j] = jnp.maximum(w[i], w[j]), jnp.minimum(w[i], w[j])
    vals = []
    for r in range(PEER_TOPK):
        m = jnp.max(w[0], axis=0, keepdims=True)
        vals.append(m)
        hit = w[0] == m
        for j in range(min(n_tiles, PEER_TOPK - 1 - r)):
            w[j] = jnp.where(hit, w[j + 1] if j + 1 < n_tiles else -jnp.inf, w[j])
    return vals


def _rows_tile(rows, iota8):
    t = jnp.zeros((SUBLANES,) + rows[0].shape[1:], F32)
    for k, r in enumerate(rows):
        t = jnp.where(iota8 == k, r, t)
    return t


_CAND_COUNTS = [PEER_TOPK // (a + 1) for a in range(SUBLANES)]


def _retrieve(s1, s2, exact):
    iota = lax.broadcasted_iota(jnp.int32, (PEER_N_KEYS, LANES), 0).astype(F32)
    iota8 = lax.broadcasted_iota(jnp.int32, (SUBLANES, LANES), 0)
    iota_c = lax.broadcasted_iota(jnp.int32, (SUBLANES * (SUBLANES + 2), LANES), 0).astype(F32)
    top = float(PEER_TOPK)
    if exact:
        v1, r1 = _top16(s1, iota, True)
        v2, r2 = _top16(s2, iota, True)
        is_rank1 = [r1 == float(a) for a in range(PEER_TOPK)]
        in_top1 = r1 < top
        repeats = 0.0
    else:
        v1 = _top16_values(s1)
        v2 = _top16_values(s2)
        is_rank1 = [s1 == v for v in v1]
        in_top1 = s1 >= v1[-1]
        repeats = sum((a == b).astype(F32) for v in (v1, v2) for a, b in zip(v[:-1], v[1:]))
        r2 = jnp.full(s2.shape, top, F32)
        for r in reversed(range(PEER_TOPK)):
            r2 = jnp.where(s2 >= v2[r], float(r), r2)
    e1 = [jnp.exp(v - v1[0]) for v in v1]
    e2 = [jnp.exp(v - v2[0]) for v in v2]
    v2_lo, v2_hi = _rows_tile(v2[:8], iota8), _rows_tile(v2[8:], iota8)
    e2_lo, e2_hi = _rows_tile(e2[:8], iota8), _rows_tile(e2[8:], iota8)
    v1_hi, e1_hi = _rows_tile(v1[8:], iota8), _rows_tile(e1[8:], iota8)
    cand, ecand = [], []
    for a, cnt in enumerate(_CAND_COUNTS):
        cand.append(jnp.where(iota8 < cnt, v1[a] + v2_lo, -jnp.inf))
        ecand.append(e1[a] * e2_lo)
        if a == 0:
            cand.append(v1[0] + v2_hi)
            ecand.append(e1[0] * e2_hi)
    cand.append(v1_hi + v2[0])
    ecand.append(e1_hi * e2[0])
    cand = jnp.concatenate(cand, axis=0)
    ecand = jnp.concatenate(ecand, axis=0)
    if exact:
        _, rc = _top16(cand, iota_c, True)
        sel = (rc < top).astype(F32)
    else:
        vc = _top16_values(cand)
        sel = (cand >= vc[-1]).astype(F32)
        repeats = repeats + sum((a == b).astype(F32) for a, b in zip(vc[:-1], vc[1:]))
    ranked = (jnp.sum(sel, axis=0, keepdims=True)
              + jnp.sum(in_top1.astype(F32), axis=0, keepdims=True)
              + jnp.sum((r2 < top).astype(F32), axis=0, keepdims=True) + repeats)
    z = jnp.sum(sel * ecand, axis=0, keepdims=True)
    n_rows = [jnp.sum(sel[0:2 * SUBLANES], axis=0, keepdims=True)]
    for a in range(1, SUBLANES):
        n_rows.append(jnp.sum(sel[SUBLANES * (a + 1):SUBLANES * (a + 2)], axis=0, keepdims=True))
    n_hi = sel[SUBLANES * (SUBLANES + 1):]
    n1 = jnp.zeros(s1.shape, F32)
    for a in range(PEER_TOPK):
        na = n_rows[a] if a < SUBLANES else n_hi[a - SUBLANES:a - SUBLANES + 1]
        n1 = jnp.where(is_rank1[a], na, n1)
    return n1, jnp.exp(s1 - v1[0]) / z, r2, jnp.exp(s2 - v2[0]), ranked


def _topk_kernel(ht_ref, wq_ref, keys_ref, n1_ref, c1_ref, b2_ref, e2_ref, qt_ref, s_ref):
    chunks = ht_ref.shape[2] // LANES
    qt_ref[...] = jnp.dot(wq_ref[...], pltpu.bitcast(ht_ref[0], BF16),
                          preferred_element_type=F32).astype(BF16)
    for hp in range(2 * PEER_HEADS):
        s_ref[hp] = jnp.dot(keys_ref[hp], qt_ref[hp * PEER_N_KEYS:(hp + 1) * PEER_N_KEYS, :],
                            preferred_element_type=F32)

    def emit(h, sl, exact):
        n1, c1, b2, e2, ranked = _retrieve(s_ref[2 * h, :, sl], s_ref[2 * h + 1, :, sl], exact)
        n1_ref[0, h, :, sl] = n1
        c1_ref[0, h, :, sl] = c1
        b2_ref[0, h, :, sl] = pltpu.bitcast(b2.astype(BF16), jnp.int32)
        e2_ref[0, h, :, sl] = pltpu.bitcast(e2.astype(BF16), jnp.int32)
        return ranked

    def pair(i, carry):
        h = i // (chunks // 2)
        c0 = (i % (chunks // 2)) * 2 * LANES
        sls = [pl.ds(pl.multiple_of(c0 + j * LANES, LANES), LANES) for j in range(2)]
        ranked = jnp.maximum(emit(h, sls[0], False), emit(h, sls[1], False))

        @pl.when(jnp.max(ranked) > float(3 * PEER_TOPK))
        def _():
            for sl in sls:
                emit(h, sl, True)

        return carry

    lax.fori_loop(0, PEER_HEADS * chunks // 2, pair, 0)


def _peer_topk(ht, wq_t, keys, tt):
    b, d2, l = ht.shape
    nqd = wq_t.shape[0]
    shp = (b, PEER_HEADS, PEER_N_KEYS, l)
    shp_packed = (b, PEER_HEADS, PEER_N_KEYS // 2, l)
    ospec = pl.BlockSpec((1, PEER_HEADS, PEER_N_KEYS, tt), lambda i, j: (i, 0, 0, j))
    ospec_packed = pl.BlockSpec((1, PEER_HEADS, PEER_N_KEYS // 2, tt), lambda i, j: (i, 0, 0, j))
    return pl.pallas_call(
        _topk_kernel,
        grid=(b, l // tt),
        in_specs=[pl.BlockSpec((1, d2, tt), lambda i, j: (i, 0, j)),
                  pl.BlockSpec((nqd, 2 * d2), lambda i, j: (0, 0)),
                  pl.BlockSpec(keys.shape, lambda i, j: (0, 0, 0))],
        out_specs=[ospec, ospec, ospec_packed, ospec_packed],
        out_shape=[jax.ShapeDtypeStruct(shp, F32), jax.ShapeDtypeStruct(shp, F32),
                   jax.ShapeDtypeStruct(shp_packed, jnp.int32),
                   jax.ShapeDtypeStruct(shp_packed, jnp.int32)],
        scratch_shapes=[pltpu.VMEM((nqd, tt), BF16),
                        pltpu.VMEM((2 * PEER_HEADS, PEER_N_KEYS, tt), F32)],
        compiler_params=_params("parallel", "parallel"),
        name="peer_topk",
    )(ht, wq_t, keys)


def _peer_kernel(ht_ref, n1_ref, c1_ref, b2_ref, e2_ref, u_ref, vt_ref, x_ref, mod_ref, o_ref,
                 a_ref, g_ref, acc_ref):
    e = pl.program_id(2)
    et = 2 * u_ref.shape[0]
    ht = pltpu.bitcast(ht_ref[0], BF16)

    @pl.when(e == 0)
    def _():
        acc_ref[...] = jnp.zeros(acc_ref.shape, F32)

    a_ref[...] = jnp.dot(pltpu.bitcast(u_ref[...], BF16), ht, preferred_element_type=F32)
    tile3 = (PEER_N_KEYS // BF16_ROWS, BF16_ROWS, PEER_GATE_LANES)
    for k in range(et // PEER_N_KEYS):
        rows = slice(k * PEER_N_KEYS, (k + 1) * PEER_N_KEYS)
        for c in range(ht.shape[1] // PEER_GATE_LANES):
            sl = slice(c * PEER_GATE_LANES, (c + 1) * PEER_GATE_LANES)
            w = jnp.zeros(tile3, BF16)
            for h in range(PEER_HEADS):
                n_t = jnp.broadcast_to(n1_ref[0, h, k:k + 1, sl], tile3[1:]).astype(BF16)[None]
                c_t = jnp.broadcast_to(c1_ref[0, h, k:k + 1, sl], tile3[1:]).astype(BF16)[None]
                b2 = pltpu.bitcast(b2_ref[0, h, :, sl], BF16).reshape(tile3)
                e2 = pltpu.bitcast(e2_ref[0, h, :, sl], BF16).reshape(tile3)
                w = w + jnp.clip(n_t - b2, 0, 1) * (e2 * c_t)
            a = a_ref[rows, sl].astype(BF16).reshape(tile3)
            z = a * (a * a * (-2.0 * _GELU_K * 0.044715 * _LOG2E) - 2.0 * _GELU_K * _LOG2E)
            act = a / (1.0 + jnp.exp2(z))
            g_ref[rows, sl] = (w * act).reshape(PEER_N_KEYS, PEER_GATE_LANES)
    acc_ref[...] += jnp.dot(pltpu.bitcast(vt_ref[...], BF16), g_ref[...],
                            preferred_element_type=F32)

    @pl.when(e == pl.num_programs(2) - 1)
    def _():
        o_ref[0] = x_ref[0] + mod_ref[0, 5:6, :] * acc_ref[...].T


def _peer_dense(ht, n1, c1, b2, e2, u, vt, x, mod, tt):
    b, d2, l = ht.shape
    d = 2 * d2
    ne = 2 * u.shape[0]
    et = PEER_EXPERTS
    mod_map = (lambda i, j, k: (i, 0, 0)) if mod.shape[0] > 1 else (lambda i, j, k: (0, 0, 0))
    sel_spec = pl.BlockSpec((1, PEER_HEADS, PEER_N_KEYS // 2, tt), lambda i, j, k: (i, 0, 0, j))
    row_spec = pl.BlockSpec((1, PEER_HEADS, et // PEER_N_KEYS, tt), lambda i, j, k: (i, 0, k, j))
    return pl.pallas_call(
        _peer_kernel,
        grid=(b, l // tt, ne // et),
        in_specs=[pl.BlockSpec((1, d2, tt), lambda i, j, k: (i, 0, j)),
                  row_spec, row_spec, sel_spec, sel_spec,
                  pl.BlockSpec((et // 2, d), lambda i, j, k: (k, 0)),
                  pl.BlockSpec((d2, et), lambda i, j, k: (0, k)),
                  pl.BlockSpec((1, tt, d), lambda i, j, k: (i, j, 0)),
                  pl.BlockSpec((1, 6, d), mod_map)],
        out_specs=pl.BlockSpec((1, tt, d), lambda i, j, k: (i, j, 0)),
        out_shape=jax.ShapeDtypeStruct((b, l, d), F32),
        scratch_shapes=[pltpu.VMEM((et, tt), F32), pltpu.VMEM((et, tt), BF16),
                        pltpu.VMEM((d, tt), F32)],
        compiler_params=_params("parallel", "parallel", "arbitrary"),
        name="peer_mix",
    )(ht, n1, c1, b2, e2, u, vt, x, mod)


def _peer(ht, x1, mod, wq_t, keys, u, vt):
    tt = min(PEER_TOKENS, ht.shape[2])
    n1, c1, b2, e2 = _peer_topk(ht, wq_t, keys, tt)
    return _peer_dense(ht, n1, c1, b2, e2, u, vt, x1, mod, min(PEER_MIX_TOKENS, ht.shape[2]))


def _rope_tables(s):
    t = np.arange(s)
    pos = np.stack([t // GRID_W, t % GRID_W], axis=1).astype(np.float32)
    axis_dim = HEAD_DIM // 2
    freqs = 1.0 / (ROPE_THETA ** (jnp.arange(0, axis_dim, 2, dtype=F32) / axis_dim))
    ang = jnp.asarray(pos)[:, :, None] * freqs[None, None, :]
    cos, sin = jnp.cos(ang), jnp.sin(ang)
    zero = jnp.zeros_like(sin)
    def lanes(first_half, second_half):
        p = jnp.stack([first_half, second_half], axis=2).reshape(s, HEAD_DIM)
        return jnp.concatenate([p, p], axis=1)
    return lanes(cos, cos), lanes(-sin, zero), lanes(zero, sin)


def _prep_kernel(u_ref, v_ref, up_ref, vtp_ref):
    up_ref[...] = pltpu.bitcast(u_ref[0].astype(BF16), jnp.int32)
    vtp_ref[...] = pltpu.bitcast(v_ref[0].T.astype(BF16), jnp.int32)


def _prep_experts(peer_u, peer_v, layer):
    _, ne, d = peer_u.shape
    te = PEER_EXPERTS // 2
    return pl.pallas_call(
        _prep_kernel,
        grid=(ne // te,),
        in_specs=[pl.BlockSpec((1, te, d), lambda i: (layer, i, 0)),
                  pl.BlockSpec((1, te, d), lambda i: (layer, i, 0))],
        out_specs=[pl.BlockSpec((te // 2, d), lambda i: (i, 0)),
                   pl.BlockSpec((d // 2, te), lambda i: (0, i))],
        out_shape=[jax.ShapeDtypeStruct((ne // 2, d), jnp.int32),
                   jax.ShapeDtypeStruct((d // 2, ne), jnp.int32)],
        compiler_params=_params("parallel"),
        name="prep_experts",
    )(peer_u, peer_v)


def _block_diag_mean():
    i = np.arange(LANES)
    return jnp.asarray((i[:, None] // HEAD_DIM == i[None, :] // HEAD_DIM) / HEAD_DIM, BF16)


def kernel(x, c, ctx, c_ctx, ada_w, ada_b, norm_g, ab_w_in, ab_w_out, a_q_norm, a_k_norm, b_q_norm,
           b_k_norm, b_lambda, b_subln, c_w_in, c_w_out, c_q_norm, c_k_norm, c_rpb, peer_w_q,
           peer_keys, peer_u, peer_v):
    bsz, s, d = x.shape
    depth = ada_w.shape[0]
    assert depth == 2, "one attention-pair layer followed by one (final) neighbourhood layer"
    log2e = math.log2(math.e)
    scale = HEAD_DIM ** -0.5 * log2e

    cs = jnp.zeros((SUBLANES, d), F32).at[:bsz].set(c).at[bsz].set(c_ctx)
    mods = _ada(cs, ada_w, ada_b)
    bd = _block_diag_mean()
    rope = _rope_tables(s)

    for l in range(depth):
        last = l == depth - 1
        mod_lat = mods[l, :bsz].reshape(bsz, 6, d)
        mod_ctx = mods[l, bsz:bsz + 1].reshape(1, 6, d)
        g1 = norm_g[l, 0].reshape(1, d)
        g2 = norm_g[l, 1].reshape(1, d)
        i = l // 2
        wq_t = peer_w_q[l].T.astype(BF16)
        keys = peer_keys[l].reshape(2 * PEER_HEADS, PEER_N_KEYS, -1).astype(BF16)
        u, vt = _prep_experts(peer_u, peer_v, l)
        if l % 2 == 0:
            lam_init = 0.8 - 0.6 * math.exp(-0.3 * l)
            qa, ka, va, qb, kb, vb = jnp.split(
                ab_w_in[i], np.cumsum([512, 128, 128, 512, 512, 512])[:-1].tolist(), axis=1)
            w = jnp.concatenate([qa, qb, kb, ka, vb, va], axis=1).astype(BF16)
            nq, nk, nv = 1024, 640, 640
            gq = jnp.concatenate([jnp.tile(a_q_norm[i], A_HEADS), jnp.tile(b_q_norm[i], 2 * B_HEADS)])
            gk = jnp.concatenate([jnp.tile(b_k_norm[i], 2 * B_HEADS), jnp.tile(a_k_norm[i], A_KV_HEADS)])
            gq = (gq * scale).reshape(1, nq)
            gk = gk.reshape(1, nk)
            q_l, kt_l, v_l = _proj(x, mod_lat, g1, w, gq, gk, bd, rope, nq, nk, nv)
            q_c, kt_c, v_c = _proj(ctx, mod_ctx, g1, w, gq, gk, bd, None, nq, nk, nv)
            lam2 = b_lambda[i]
            sub = b_subln[i].reshape(1, LANES)
            kt_all = jnp.concatenate([kt_l, kt_c], axis=2)
            v_all = jnp.concatenate([v_l, v_c], axis=1)
            oa = _gqa(q_l, kt_all, v_all)
            ob = _diff(q_l, kt_all, v_all, lam2, sub, lam_init)
            w_out = ab_w_out[i].astype(BF16)
            x1, ht = _out(oa, 0, ob, 0, w_out, x, mod_lat, g2)
            if not last:
                oa_c = _gqa(q_c, kt_c, v_c)
                ob_c = _diff(q_c, kt_c, v_c, lam2, sub, lam_init)
                ctx1, ht_c = _out(oa_c, 0, ob_c, 0, w_out, ctx, mod_ctx, g2)
        else:
            w = c_w_in[i].astype(BF16)
            nq = nk = nv = C_HEADS * HEAD_DIM
            gq = (jnp.tile(c_q_norm[i], C_HEADS) * scale).reshape(1, nq)
            gk = jnp.tile(c_k_norm[i], C_HEADS).reshape(1, nk)
            q_l, kt_l, v_l = _proj(x, mod_lat, g1, w, gq, gk, bd, None, nq, nk, nv)
            q_c, kt_c, v_c = _proj(ctx, mod_ctx, g1, w, gq, gk, bd, None, nq, nk, nv)
            o = _na(q_l, kt_l, v_l, kt_c, v_c, _na_bias_table(c_rpb[i] * log2e))
            w_out = c_w_out[i].astype(BF16)
            x1, ht = _out(o, 0, o, 1, w_out, x, mod_lat, g2)
        x = _peer(ht, x1, mod_lat, wq_t, keys, u, vt)
        if not last:
            n_ctx = ctx1.shape[1]
            ht_all = ht_c.transpose(1, 0, 2).reshape(1, d // 2, bsz * n_ctx)
            ctx = _peer(ht_all, ctx1.reshape(1, bsz * n_ctx, d), mod_ctx, wq_t, keys, u, vt)
            ctx = ctx.reshape(bsz, n_ctx, d)
    return x
```

```python
import functools
import math

import numpy as np
import jax
import jax.numpy as jnp
from jax import lax
from jax.experimental import pallas as pl
from jax.experimental.pallas import tpu as pltpu

F32 = jnp.float32
BF16 = jnp.bfloat16

GRID_W = 64
HEAD_DIM = 64
ROPE_THETA = 10000.0
EPS = 1e-6
A_HEADS = 8
A_KV_HEADS = 2
B_HEADS = 4
C_HEADS = 16
NA_ROWS = 8
NA_COLS = 16
PEER_HEADS = 8
PEER_N_KEYS = 128
PEER_TOPK = 16

LANES = 128
SUBLANES = 8
BF16_ROWS = 16
VMEM_LIMIT = 56 * 1024 * 1024
NEG = -1e30

ADA_COLS = 1536
ROW_TILE = 512
Q_TILE = 256
GQA_SCORE_ELEMS = 3072 * 1024
DIFF_SCORE_ELEMS = 1536 * 1024
NA_BLOCK_ROWS = 4
PEER_TOKENS = 512
PEER_MIX_TOKENS = 1024
PEER_EXPERTS = 1024
PEER_GATE_LANES = LANES
PEER_PROJ_BLOCKS = 4
_GELU_K = math.sqrt(2.0 / math.pi)
_LOG2E = math.log2(math.e)


def _params(*sem):
    return pltpu.CompilerParams(dimension_semantics=sem, vmem_limit_bytes=VMEM_LIMIT)


def _ada_kernel(cs_ref, w_ref, b_ref, o_ref):
    cs = cs_ref[...]
    a = cs * jax.nn.sigmoid(cs)
    o_ref[0] = jnp.dot(a, w_ref[0], preferred_element_type=F32,
                       precision=lax.Precision.HIGHEST) + b_ref[0]


def _ada(cs, ada_w, ada_b):
    depth, d, n = ada_w.shape
    tn = ADA_COLS
    return pl.pallas_call(
        _ada_kernel,
        grid=(depth, n // tn),
        in_specs=[pl.BlockSpec((SUBLANES, d), lambda l, j: (0, 0)),
                  pl.BlockSpec((1, d, tn), lambda l, j: (l, 0, j)),
                  pl.BlockSpec((1, 1, tn), lambda l, j: (l, 0, j))],
        out_specs=pl.BlockSpec((1, SUBLANES, tn), lambda l, j: (l, 0, j)),
        out_shape=jax.ShapeDtypeStruct((depth, SUBLANES, n), F32),
        compiler_params=_params("parallel", "parallel"),
        name="ada_mod",
    )(cs, ada_w, ada_b.reshape(depth, 1, n))


def _modulated_norm(x, gain, shift, scale):
    ms = jnp.mean(x * x, axis=-1, keepdims=True)
    return (x * lax.rsqrt(ms + EPS)) * gain * (1.0 + scale) + shift


def _proj_kernel(*refs, nq, nk, rope):
    if rope:
        (x_ref, mod_ref, g_ref, w_ref, gq_ref, gk_ref, bd_ref, cos_ref, sa_ref, sb_ref,
         q_ref, kt_ref, v_ref) = refs
    else:
        x_ref, mod_ref, g_ref, w_ref, gq_ref, gk_ref, bd_ref, q_ref, kt_ref, v_ref = refs
    h = _modulated_norm(x_ref[0], g_ref[...], mod_ref[0, 0:1, :], mod_ref[0, 1:2, :])
    p = jnp.dot(h.astype(BF16), w_ref[...], preferred_element_type=F32)
    bd = bd_ref[...]

    def head_norm(c0, gain):
        yc = p[:, c0:c0 + LANES]
        ss = jnp.dot((yc * yc).astype(BF16), bd, preferred_element_type=F32)
        yn = yc * lax.rsqrt(ss + EPS) * gain
        if rope:
            yn = (yn * cos_ref[...] + pltpu.roll(yn, LANES - 16, 1) * sa_ref[...]
                  + pltpu.roll(yn, 16, 1) * sb_ref[...])
        return yn

    for c in range(nq // LANES):
        c0 = c * LANES
        q_ref[0, :, c0:c0 + LANES] = head_norm(c0, gq_ref[:, c0:c0 + LANES]).astype(BF16)
    for c in range(nk // LANES):
        c0 = c * LANES
        yn = head_norm(nq + c0, gk_ref[:, c0:c0 + LANES])
        kt_ref[0, c0:c0 + LANES, :] = yn.T.astype(BF16)
    ones = jnp.ones((p.shape[0], LANES), BF16)
    for c in range((p.shape[1] - nq - nk) // LANES):
        c0 = nq + nk + c * LANES
        v_ref[0, :, 2 * c * LANES:(2 * c + 1) * LANES] = p[:, c0:c0 + LANES].astype(BF16)
        v_ref[0, :, (2 * c + 1) * LANES:(2 * c + 2) * LANES] = ones


def _proj(x, mod, gain, w, gq, gk, bd, rope_tabs, nq, nk, nv):
    b, l, d = x.shape
    tm = min(ROW_TILE, l)
    n = nq + nk + nv
    mod_map = (lambda i, j: (i, 0, 0)) if mod.shape[0] > 1 else (lambda i, j: (0, 0, 0))
    in_specs = [pl.BlockSpec((1, tm, d), lambda i, j: (i, j, 0)),
                pl.BlockSpec((1, 6, d), mod_map),
                pl.BlockSpec((1, d), lambda i, j: (0, 0)),
                pl.BlockSpec((d, n), lambda i, j: (0, 0)),
                pl.BlockSpec((1, nq), lambda i, j: (0, 0)),
                pl.BlockSpec((1, nk), lambda i, j: (0, 0)),
                pl.BlockSpec((LANES, LANES), lambda i, j: (0, 0))]
    args = [x, mod, gain, w, gq, gk, bd]
    if rope_tabs is not None:
        in_specs += [pl.BlockSpec((tm, LANES), lambda i, j: (j, 0))] * 3
        args += list(rope_tabs)
    return pl.pallas_call(
        functools.partial(_proj_kernel, nq=nq, nk=nk, rope=rope_tabs is not None),
        grid=(b, l // tm),
        in_specs=in_specs,
        out_specs=[pl.BlockSpec((1, tm, nq), lambda i, j: (i, j, 0)),
                   pl.BlockSpec((1, nk, tm), lambda i, j: (i, 0, j)),
                   pl.BlockSpec((1, tm, 2 * nv), lambda i, j: (i, j, 0))],
        out_shape=[jax.ShapeDtypeStruct((b, l, nq), BF16),
                   jax.ShapeDtypeStruct((b, nk, l), BF16),
                   jax.ShapeDtypeStruct((b, l, 2 * nv), BF16)],
        compiler_params=_params("parallel", "parallel"),
        name="in_proj",
    )(*args)


def _online_update(g, s, v, m_ref, acc_ref):
    m_prev = m_ref[g]
    m_new = jnp.maximum(m_prev, jnp.max(s, axis=-1, keepdims=True))
    alpha = jnp.exp2(m_prev - m_new)
    p = jnp.exp2(s - m_new)
    acc_ref[g] = alpha * acc_ref[g] + jnp.dot(p.astype(BF16), v, preferred_element_type=F32)
    m_ref[g] = m_new


def _normalised(acc_ref, g, rows):
    return acc_ref[g, rows, :LANES] / acc_ref[g, rows, LANES:LANES + 1]


def _attn_sweep(kt_rows, v_lane, qs_ref, kt_ref, v_ref, s_ref, m_ref, acc_ref):
    tq = qs_ref.shape[1] // len(kt_rows[0])
    m_ref[...] = jnp.full(m_ref.shape, -jnp.inf, F32)
    acc_ref[...] = jnp.zeros(acc_ref.shape, F32)

    def scores(g, kt):
        rows = kt_rows[g]
        if len(set(rows)) == 1:
            return jnp.dot(qs_ref[g], kt(rows[0]), preferred_element_type=F32)
        return jnp.concatenate(
            [jnp.dot(qs_ref[g, i * tq:(i + 1) * tq, :], kt(r), preferred_element_type=F32)
             for i, r in enumerate(rows)], axis=0)

    tk = s_ref.shape[2]
    n_chunks = kt_ref.shape[2] // tk

    def kt_chunk(c):
        off = pl.multiple_of(c * tk, LANES)
        return lambda r: kt_ref[0, r:r + HEAD_DIM, pl.ds(off, tk)]

    def v_chunk(c, g):
        off = pl.multiple_of(c * tk, LANES)
        return v_ref[0, pl.ds(off, tk), v_lane[g]:v_lane[g] + 2 * LANES]

    s_ref[0] = scores(0, kt_chunk(0))

    def body(c, carry):
        s_ref[1] = scores(1, kt_chunk(c))
        _online_update(0, s_ref[0], v_chunk(c, 0), m_ref, acc_ref)
        s_ref[0] = scores(0, kt_chunk(jnp.minimum(c + 1, n_chunks - 1)))
        _online_update(1, s_ref[1], v_chunk(c, 1), m_ref, acc_ref)
        return carry

    lax.fori_loop(0, n_chunks, body, 0)


def _gqa_kernel(q_ref, kt_ref, v_ref, o_ref, qs_ref, m_ref, acc_ref, s_ref):
    tq = q_ref.shape[1]
    group = A_HEADS // A_KV_HEADS
    for h in range(A_HEADS):
        qs_ref[h // group, (h % group) * tq:(h % group + 1) * tq, :] = (
            q_ref[0, :, h * HEAD_DIM:(h + 1) * HEAD_DIM])
    _attn_sweep([[g * HEAD_DIM] * group for g in range(A_KV_HEADS)], [0, 0],
                qs_ref, kt_ref, v_ref, s_ref, m_ref, acc_ref)
    lane = lax.broadcasted_iota(jnp.int32, (tq, LANES), 1)
    for j in range(A_HEADS // 2):
        g, i = (2 * j) // group, (2 * j) % group
        rows_a = slice(i * tq, (i + 1) * tq)
        rows_c = slice((i + 1) * tq, (i + 2) * tq)
        a = _normalised(acc_ref, g, rows_a)
        c = _normalised(acc_ref, g, rows_c)
        if g == 0:
            c = pltpu.roll(c, HEAD_DIM, 1)
        else:
            a = pltpu.roll(a, HEAD_DIM, 1)
        o_ref[0, :, j * LANES:(j + 1) * LANES] = jnp.where(lane < HEAD_DIM, a, c).astype(BF16)


def _diff_kernel(q_ref, kt_ref, v_ref, lam_ref, sub_ref, o_ref, qs_ref, m_ref, acc_ref, s_ref,
                 *, lam_init):
    tq = q_ref.shape[1]
    for g in range(2):
        for i in range(2):
            c0 = g * LANES + i * HEAD_DIM
            qs_ref[g, i * tq:(i + 1) * tq, :] = q_ref[0, :, c0:c0 + HEAD_DIM]
    _attn_sweep([[g * LANES, g * LANES + HEAD_DIM] for g in range(2)], [0, 2 * LANES],
                qs_ref, kt_ref, v_ref, s_ref, m_ref, acc_ref)
    lf = lam_ref[...]
    lam = (jnp.exp(jnp.sum(lf[0:1] * lf[1:2], axis=-1, keepdims=True))
           - jnp.exp(jnp.sum(lf[2:3] * lf[3:4], axis=-1, keepdims=True)) + lam_init)
    for g in range(2):
        o = (_normalised(acc_ref, g, slice(0, tq))
             - lam * _normalised(acc_ref, g, slice(tq, 2 * tq)))
        ms = jnp.mean(o * o, axis=-1, keepdims=True)
        o = (o * lax.rsqrt(ms + EPS)) * sub_ref[...] * (1.0 - lam_init)
        o_ref[0, :, g * LANES:(g + 1) * LANES] = o.astype(BF16)


def _attn_scratch(rows, tk):
    return [pltpu.VMEM((2, rows, HEAD_DIM), BF16), pltpu.VMEM((2, rows, 1), F32),
            pltpu.VMEM((2, rows, 2 * LANES), F32), pltpu.VMEM((2, rows, tk), F32)]


def _key_chunk(n_keys, rows, budget):
    cap = min(budget // rows, n_keys)
    return max(c for c in range(LANES, cap + 1, LANES) if n_keys % c == 0)


_KV_A_BLOCK = B_HEADS * 2 * HEAD_DIM // LANES


def _gqa(q, kt, v):
    b, lq, _ = q.shape
    tq = min(Q_TILE, lq)
    n_keys = kt.shape[2]
    blk = _KV_A_BLOCK
    return pl.pallas_call(
        _gqa_kernel,
        grid=(b, lq // tq),
        in_specs=[pl.BlockSpec((1, tq, A_HEADS * HEAD_DIM), lambda i, j: (i, j, 0)),
                  pl.BlockSpec((1, LANES, n_keys), lambda i, j: (i, blk, 0)),
                  pl.BlockSpec((1, n_keys, 2 * LANES), lambda i, j: (i, 0, blk))],
        out_specs=pl.BlockSpec((1, tq, A_HEADS * HEAD_DIM), lambda i, j: (i, j, 0)),
        out_shape=jax.ShapeDtypeStruct((b, lq, A_HEADS * HEAD_DIM), BF16),
        scratch_shapes=_attn_scratch(A_HEADS // A_KV_HEADS * tq,
                                     _key_chunk(n_keys, A_HEADS // A_KV_HEADS * tq, GQA_SCORE_ELEMS)),
        compiler_params=_params("parallel", "parallel"),
        name="gqa_attn",
    )(q, kt, v)


def _diff(q, kt, v, b_lambda, b_subln, lam_init):
    b, lq, _ = q.shape
    tq = min(2 * Q_TILE, lq)
    n_keys = kt.shape[2]
    w = 2 * LANES
    q0 = A_HEADS * HEAD_DIM // w
    return pl.pallas_call(
        functools.partial(_diff_kernel, lam_init=lam_init),
        grid=(b, B_HEADS // 2, lq // tq),
        in_specs=[pl.BlockSpec((1, tq, w), lambda i, h, j: (i, j, q0 + h)),
                  pl.BlockSpec((1, w, n_keys), lambda i, h, j: (i, h, 0)),
                  pl.BlockSpec((1, n_keys, 2 * w), lambda i, h, j: (i, 0, h)),
                  pl.BlockSpec((4, HEAD_DIM), lambda i, h, j: (0, 0)),
                  pl.BlockSpec((1, LANES), lambda i, h, j: (0, 0))],
        out_specs=pl.BlockSpec((1, tq, w), lambda i, h, j: (i, j, h)),
        out_shape=jax.ShapeDtypeStruct((b, lq, B_HEADS * LANES), BF16),
        scratch_shapes=_attn_scratch(2 * tq, _key_chunk(n_keys, 2 * tq, DIFF_SCORE_ELEMS)),
        compiler_params=_params("parallel", "parallel", "parallel"),
        name="diff_attn",
    )(q, kt, v, b_lambda, b_subln)


def _na_kernel(q_ref, km_ref, k0_ref, kp_ref, kc_ref, vm_ref, v0_ref, vp_ref, vc_ref, bias_ref,
               o_ref, *, rows):
    i = pl.program_id(1)
    tq = q_ref.shape[1]
    nb = 3 * tq
    qrow = NA_BLOCK_ROWS * i + lax.broadcasted_iota(jnp.int32, (tq, nb), 0) // GRID_W
    krow = NA_BLOCK_ROWS * (i - 1) + lax.broadcasted_iota(jnp.int32, (tq, nb), 1) // GRID_W
    rs = jnp.clip(qrow - NA_ROWS // 2, 0, rows - NA_ROWS)
    row_mask = jnp.where((krow >= rs) & (krow < rs + NA_ROWS), 0.0, NEG).astype(F32)
    lane = lax.broadcasted_iota(jnp.int32, (tq, LANES), 1)
    kts = (km_ref, k0_ref, kp_ref)
    vs = (vm_ref, v0_ref, vp_ref)
    for pair in range(C_HEADS // 2):
        outs = []
        for h in (2 * pair, 2 * pair + 1):
            r0 = h * HEAD_DIM
            q = q_ref[0, :, r0:r0 + HEAD_DIM]
            s_band = jnp.concatenate(
                [jnp.dot(q, k[0, r0:r0 + HEAD_DIM, :], preferred_element_type=F32) for k in kts],
                axis=1) + bias_ref[h] + row_mask
            s_ctx = jnp.dot(q, kc_ref[0, r0:r0 + HEAD_DIM, :], preferred_element_type=F32)
            m = jnp.maximum(jnp.max(s_band, axis=-1, keepdims=True),
                            jnp.max(s_ctx, axis=-1, keepdims=True))
            p_band = jnp.exp2(s_band - m)
            p_ctx = jnp.exp2(s_ctx - m)
            c0 = pair * 2 * LANES
            pv = jnp.dot(p_ctx.astype(BF16), vc_ref[0, :, c0:c0 + 2 * LANES],
                         preferred_element_type=F32)
            for t, v in enumerate(vs):
                pv += jnp.dot(p_band[:, t * tq:(t + 1) * tq].astype(BF16),
                              v[0, :, c0:c0 + 2 * LANES], preferred_element_type=F32)
            outs.append(pv[:, :LANES] / pv[:, LANES:LANES + 1])
        a, c = outs
        o_ref[0, :, pair * LANES:(pair + 1) * LANES] = jnp.where(lane < HEAD_DIM, a, c).astype(BF16)


def _na(q, kt_lat, v_lat, kt_ctx, v_ctx, bias):
    b, s, w = q.shape
    tq = NA_BLOCK_ROWS * GRID_W
    nblk = s // tq
    nc = kt_ctx.shape[2]
    rows = s // GRID_W

    def kt_spec(d):
        return pl.BlockSpec((1, w, tq), lambda i, j: (i, 0, jnp.clip(j + d, 0, nblk - 1)))

    wv = v_lat.shape[2]

    def v_spec(d):
        return pl.BlockSpec((1, tq, wv), lambda i, j: (i, jnp.clip(j + d, 0, nblk - 1), 0))

    return pl.pallas_call(
        functools.partial(_na_kernel, rows=rows),
        grid=(b, nblk),
        in_specs=[pl.BlockSpec((1, tq, w), lambda i, j: (i, j, 0)),
                  kt_spec(-1), kt_spec(0), kt_spec(1),
                  pl.BlockSpec((1, w, nc), lambda i, j: (i, 0, 0)),
                  v_spec(-1), v_spec(0), v_spec(1),
                  pl.BlockSpec((1, nc, wv), lambda i, j: (i, 0, 0)),
                  pl.BlockSpec(bias.shape, lambda i, j: (0, 0, 0))],
        out_specs=pl.BlockSpec((1, tq, w), lambda i, j: (i, j, 0)),
        out_shape=jax.ShapeDtypeStruct((b, s, w), BF16),
        compiler_params=_params("parallel", "parallel"),
        name="na_attn",
    )(q, kt_lat, kt_lat, kt_lat, kt_ctx, v_lat, v_lat, v_lat, v_ctx, bias)


def _na_bias_table(rpb):
    heads = rpb.shape[0]
    nr, nc = 2 * NA_ROWS - 1, 2 * NA_COLS - 1
    j = np.arange(GRID_W)
    dc = j[None, :] - j[:, None] + (NA_COLS - 1)
    cs = np.clip(j - NA_COLS // 2, 0, GRID_W - NA_COLS)
    col_ok = (j[None, :] >= cs[:, None]) & (j[None, :] < cs[:, None] + NA_COLS)
    onehot = (np.arange(nc)[:, None, None] == dc[None]) & col_ok[None]
    cols = jnp.einsum('hrm,mjk->hrjk', rpb, jnp.asarray(onehot, F32),
                      precision=lax.Precision.HIGHEST)
    cols = jnp.where(col_ok[None, None], cols, NEG)
    band = 3 * NA_BLOCK_ROWS
    per_q = [cols[:, NA_ROWS - 1 - NA_BLOCK_ROWS - qr:NA_ROWS - 1 - NA_BLOCK_ROWS - qr + band]
             for qr in range(NA_BLOCK_ROWS)]
    t = jnp.stack(per_q, axis=1)
    assert t.shape == (heads, NA_BLOCK_ROWS, band, GRID_W, GRID_W) and nr == band + NA_BLOCK_ROWS - 1
    return t.transpose(0, 1, 3, 2, 4).reshape(heads, NA_BLOCK_ROWS * GRID_W, band * GRID_W)


def _out_kernel(o1_ref, o2_ref, w_ref, x_ref, mod_ref, g_ref, x1_ref, ht_ref):
    half = o1_ref.shape[2]
    o = (jnp.dot(o1_ref[0], w_ref[0:half, :], preferred_element_type=F32)
         + jnp.dot(o2_ref[0], w_ref[half:, :], preferred_element_type=F32))
    x1 = x_ref[0] + mod_ref[0, 2:3, :] * o
    x1_ref[0] = x1
    h2 = _modulated_norm(x1, g_ref[...], mod_ref[0, 3:4, :], mod_ref[0, 4:5, :])
    ht_ref[0] = pltpu.bitcast(h2.T.astype(BF16), jnp.int32)


def _out(o1, o1_blk, o2, o2_blk, w, x, mod, gain):
    b, l, d = x.shape
    tm = min(ROW_TILE, l)
    half = d // 2
    mod_map = (lambda i, j: (i, 0, 0)) if mod.shape[0] > 1 else (lambda i, j: (0, 0, 0))
    return pl.pallas_call(
        _out_kernel,
        grid=(b, l // tm),
        in_specs=[pl.BlockSpec((1, tm, half), lambda i, j: (i, j, o1_blk)),
                  pl.BlockSpec((1, tm, half), lambda i, j: (i, j, o2_blk)),
                  pl.BlockSpec((d, d), lambda i, j: (0, 0)),
                  pl.BlockSpec((1, tm, d), lambda i, j: (i, j, 0)),
                  pl.BlockSpec((1, 6, d), mod_map),
                  pl.BlockSpec((1, d), lambda i, j: (0, 0))],
        out_specs=[pl.BlockSpec((1, tm, d), lambda i, j: (i, j, 0)),
                   pl.BlockSpec((1, d // 2, tm), lambda i, j: (i, 0, j))],
        out_shape=[jax.ShapeDtypeStruct((b, l, d), F32),
                   jax.ShapeDtypeStruct((b, d // 2, l), jnp.int32)],
        compiler_params=_params("parallel", "parallel"),
        name="out_proj",
    )(o1, o2, w, x, mod, gain)


def _top16(s, iota, exact):
    rank = jnp.full(s.shape, float(PEER_TOPK), F32)
    big = float(s.shape[0])
    vals = []
    for r in range(PEER_TOPK):
        m = jnp.max(s, axis=0, keepdims=True)
        hit = s == m
        if exact:
            hit = iota == jnp.min(jnp.where(hit, iota, big), axis=0, keepdims=True)
        rank = jnp.where(hit, float(r), rank)
        s = jnp.where(hit, -jnp.inf, s)
        vals.append(m)
    return vals, rank


def _sorting_network(n):
    pairs = []
    p = 1
    while p < n:
        k = p
        while k >= 1:
            for j in range(k % p, n - k, 2 * k):
                for i in range(min(k, n - j - k)):
                    if (i + j) // (2 * p) == (i + j + k) // (2 * p):
                        pairs.append((i + j, i + j + k))
            k //= 2
        p *= 2
    return pairs


def _top16_values(s):
    n_tiles = s.shape[0] // SUBLANES
    w = [s[SUBLANES * j:SUBLANES * (j + 1)] for j in range(n_tiles)]
    for i, j in _sorting_network(PEER_TOPK):
        if j < n_tiles:
            w[i], w[j] = jnp.maximum(w[i], w[j]), jnp.minimum(w[i], w[j])
    vals = []
    for r in range(PEER_TOPK):
        m = jnp.max(w[0], axis=0, keepdims=True)
        vals.append(m)
        hit = w[0] == m
        for j in range(min(n_tiles, PEER_TOPK - 1 - r)):
            w[j] = jnp.where(hit, w[j + 1] if j + 1 < n_tiles else -jnp.inf, w[j])
    return vals


def _rows_tile(rows, iota8):
    t = jnp.zeros((SUBLANES,) + rows[0].shape[1:], F32)
    for k, r in enumerate(rows):
        t = jnp.where(iota8 == k, r, t)
    return t


_CAND_COUNTS = [PEER_TOPK // (a + 1) for a in range(SUBLANES)]


def _retrieve(s1, s2, exact):
    iota = lax.broadcasted_iota(jnp.int32, (PEER_N_KEYS, LANES), 0).astype(F32)
    iota8 = lax.broadcasted_iota(jnp.int32, (SUBLANES, LANES), 0)
    iota_c = lax.broadcasted_iota(jnp.int32, (SUBLANES * (SUBLANES + 2), LANES), 0).astype(F32)
    top = float(PEER_TOPK)
    if exact:
        v1, r1 = _top16(s1, iota, True)
        v2, r2 = _top16(s2, iota, True)
        is_rank1 = [r1 == float(a) for a in range(PEER_TOPK)]
        in_top1 = r1 < top
        repeats = 0.0
    else:
        v1 = _top16_values(s1)
        v2 = _top16_values(s2)
        is_rank1 = [s1 == v for v in v1]
        in_top1 = s1 >= v1[-1]
        repeats = sum((a == b).astype(F32) for v in (v1, v2) for a, b in zip(v[:-1], v[1:]))
        r2 = jnp.full(s2.shape, top, F32)
        for r in reversed(range(PEER_TOPK)):
            r2 = jnp.where(s2 >= v2[r], float(r), r2)
    e1 = [jnp.exp(v - v1[0]) for v in v1]
    e2 = [jnp.exp(v - v2[0]) for v in v2]
    v2_lo, v2_hi = _rows_tile(v2[:8], iota8), _rows_tile(v2[8:], iota8)
    e2_lo, e2_hi = _rows_tile(e2[:8], iota8), _rows_tile(e2[8:], iota8)
    v1_hi, e1_hi = _rows_tile(v1[8:], iota8), _rows_tile(e1[8:], iota8)
    cand, ecand = [], []
    for a, cnt in enumerate(_CAND_COUNTS):
        cand.append(jnp.where(iota8 < cnt, v1[a] + v2_lo, -jnp.inf))
        ecand.append(e1[a] * e2_lo)
        if a == 0:
            cand.append(v1[0] + v2_hi)
            ecand.append(e1[0] * e2_hi)
    cand.append(v1_hi + v2[0])
    ecand.append(e1_hi * e2[0])
    cand = jnp.concatenate(cand, axis=0)
    ecand = jnp.concatenate(ecand, axis=0)
    if exact:
        _, rc = _top16(cand, iota_c, True)
        sel = (rc < top).astype(F32)
    else:
        vc = _top16_values(cand)
        sel = (cand >= vc[-1]).astype(F32)
        repeats = repeats + sum((a == b).astype(F32) for a, b in zip(vc[:-1], vc[1:]))
    ranked = (jnp.sum(sel, axis=0, keepdims=True)
              + jnp.sum(in_top1.astype(F32), axis=0, keepdims=True)
              + jnp.sum((r2 < top).astype(F32), axis=0, keepdims=True) + repeats)
    z = jnp.sum(sel * ecand, axis=0, keepdims=True)
    n_rows = [jnp.sum(sel[0:2 * SUBLANES], axis=0, keepdims=True)]
    for a in range(1, SUBLANES):
        n_rows.append(jnp.sum(sel[SUBLANES * (a + 1):SUBLANES * (a + 2)], axis=0, keepdims=True))
    n_hi = sel[SUBLANES * (SUBLANES + 1):]
    n1 = jnp.zeros(s1.shape, F32)
    for a in range(PEER_TOPK):
        na = n_rows[a] if a < SUBLANES else n_hi[a - SUBLANES:a - SUBLANES + 1]
        n1 = jnp.where(is_rank1[a], na, n1)
    return n1, jnp.exp(s1 - v1[0]) / z, r2, jnp.exp(s2 - v2[0]), ranked


def _topk_kernel(ht_ref, wq_ref, keys_ref, n1_ref, c1_ref, b2_ref, e2_ref, qt_ref, s_ref):
    chunks = ht_ref.shape[2] // LANES
    qt_ref[...] = jnp.dot(wq_ref[...], pltpu.bitcast(ht_ref[0], BF16),
                          preferred_element_type=F32).astype(BF16)
    for hp in range(2 * PEER_HEADS):
        s_ref[hp] = jnp.dot(keys_ref[hp], qt_ref[hp * PEER_N_KEYS:(hp + 1) * PEER_N_KEYS, :],
                            preferred_element_type=F32)

    def emit(h, sl, exact):
        n1, c1, b2, e2, ranked = _retrieve(s_ref[2 * h, :, sl], s_ref[2 * h + 1, :, sl], exact)
        n1_ref[0, h, :, sl] = n1
        c1_ref[0, h, :, sl] = c1
        b2_ref[0, h, :, sl] = pltpu.bitcast(b2.astype(BF16), jnp.int32)
        e2_ref[0, h, :, sl] = pltpu.bitcast(e2.astype(BF16), jnp.int32)
        return ranked

    def pair(i, carry):
        h = i // (chunks // 2)
        c0 = (i % (chunks // 2)) * 2 * LANES
        sls = [pl.ds(pl.multiple_of(c0 + j * LANES, LANES), LANES) for j in range(2)]
        ranked = jnp.maximum(emit(h, sls[0], False), emit(h, sls[1], False))

        @pl.when(jnp.max(ranked) > float(3 * PEER_TOPK))
        def _():
            for sl in sls:
                emit(h, sl, True)

        return carry

    lax.fori_loop(0, PEER_HEADS * chunks // 2, pair, 0)


def _peer_topk(ht, wq_t, keys, tt):
    b, d2, l = ht.shape
    nqd = wq_t.shape[0]
    shp = (b, PEER_HEADS, PEER_N_KEYS, l)
    shp_packed = (b, PEER_HEADS, PEER_N_KEYS // 2, l)
    ospec = pl.BlockSpec((1, PEER_HEADS, PEER_N_KEYS, tt), lambda i, j: (i, 0, 0, j))
    ospec_packed = pl.BlockSpec((1, PEER_HEADS, PEER_N_KEYS // 2, tt), lambda i, j: (i, 0, 0, j))
    return pl.pallas_call(
        _topk_kernel,
        grid=(b, l // tt),
        in_specs=[pl.BlockSpec((1, d2, tt), lambda i, j: (i, 0, j)),
                  pl.BlockSpec((nqd, 2 * d2), lambda i, j: (0, 0)),
                  pl.BlockSpec(keys.shape, lambda i, j: (0, 0, 0))],
        out_specs=[ospec, ospec, ospec_packed, ospec_packed],
        out_shape=[jax.ShapeDtypeStruct(shp, F32), jax.ShapeDtypeStruct(shp, F32),
                   jax.ShapeDtypeStruct(shp_packed, jnp.int32),
                   jax.ShapeDtypeStruct(shp_packed, jnp.int32)],
        scratch_shapes=[pltpu.VMEM((nqd, tt), BF16),
                        pltpu.VMEM((2 * PEER_HEADS, PEER_N_KEYS, tt), F32)],
        compiler_params=_params("parallel", "parallel"),
        name="peer_topk",
    )(ht, wq_t, keys)


def _peer_kernel(ht_ref, n1_ref, c1_ref, b2_ref, e2_ref, u_ref, vt_ref, x_ref, mod_ref, o_ref,
                 a_ref, g_ref, acc_ref):
    e = pl.program_id(2)
    et = 2 * u_ref.shape[0]
    ht = pltpu.bitcast(ht_ref[0], BF16)

    @pl.when(e == 0)
    def _():
        acc_ref[...] = jnp.zeros(acc_ref.shape, F32)

    tile3 = (PEER_N_KEYS // BF16_ROWS, BF16_ROWS, PEER_GATE_LANES)
    d = u_ref.shape[1]
    blk_rows = et // PEER_PROJ_BLOCKS
    hold = None
    for blk in range(PEER_PROJ_BLOCKS):
        u_blk = pltpu.bitcast(u_ref[blk * blk_rows // 2:(blk + 1) * blk_rows // 2, :], BF16)
        if hold is not None:
            u_blk = (u_blk.reshape(blk_rows // BF16_ROWS, BF16_ROWS, d) + hold).reshape(blk_rows, d)
        a_ref[blk * blk_rows:(blk + 1) * blk_rows, :] = jnp.dot(u_blk, ht,
                                                                preferred_element_type=F32)
        next_hold = None
        for k in range(blk * blk_rows // PEER_N_KEYS, (blk + 1) * blk_rows // PEER_N_KEYS):
            rows = slice(k * PEER_N_KEYS, (k + 1) * PEER_N_KEYS)
            for c in range(ht.shape[1] // PEER_GATE_LANES):
                sl = slice(c * PEER_GATE_LANES, (c + 1) * PEER_GATE_LANES)
                w = jnp.zeros(tile3, BF16)
                for h in range(PEER_HEADS):
                    n_t = jnp.broadcast_to(n1_ref[0, h, k:k + 1, sl], tile3[1:]).astype(BF16)[None]
                    c_t = jnp.broadcast_to(c1_ref[0, h, k:k + 1, sl], tile3[1:]).astype(BF16)[None]
                    b2 = pltpu.bitcast(b2_ref[0, h, :, sl], BF16).reshape(tile3)
                    e2 = pltpu.bitcast(e2_ref[0, h, :, sl], BF16).reshape(tile3)
                    w = w + jnp.clip(n_t - b2, 0, 1) * (e2 * c_t)
                if next_hold is None:
                    zero = w[0] * jnp.zeros((), BF16)
                    next_hold = jnp.concatenate([zero] * (d // PEER_GATE_LANES), axis=1)[None]
                a = a_ref[rows, sl].astype(BF16).reshape(tile3)
                z = a * (a * a * (-2.0 * _GELU_K * 0.044715 * _LOG2E) - 2.0 * _GELU_K * _LOG2E)
                act = a / (1.0 + jnp.exp2(z))
                g_ref[rows, sl] = (w * act).reshape(PEER_N_KEYS, PEER_GATE_LANES)
        hold = next_hold
    acc_ref[...] += jnp.dot(pltpu.bitcast(vt_ref[...], BF16), g_ref[...],
                            preferred_element_type=F32)

    @pl.when(e == pl.num_programs(2) - 1)
    def _():
        o_ref[0] = x_ref[0] + mod_ref[0, 5:6, :] * acc_ref[...].T


def _peer_dense(ht, n1, c1, b2, e2, u, vt, x, mod, tt):
    b, d2, l = ht.shape
    d = 2 * d2
    ne = 2 * u.shape[0]
    et = PEER_EXPERTS
    mod_map = (lambda i, j, k: (i, 0, 0)) if mod.shape[0] > 1 else (lambda i, j, k: (0, 0, 0))
    sel_spec = pl.BlockSpec((1, PEER_HEADS, PEER_N_KEYS // 2, tt), lambda i, j, k: (i, 0, 0, j))
    row_spec = pl.BlockSpec((1, PEER_HEADS, et // PEER_N_KEYS, tt), lambda i, j, k: (i, 0, k, j))
    return pl.pallas_call(
        _peer_kernel,
        grid=(b, l // tt, ne // et),
        in_specs=[pl.BlockSpec((1, d2, tt), lambda i, j, k: (i, 0, j)),
                  row_spec, row_spec, sel_spec, sel_spec,
                  pl.BlockSpec((et // 2, d), lambda i, j, k: (k, 0)),
                  pl.BlockSpec((d2, et), lambda i, j, k: (0, k)),
                  pl.BlockSpec((1, tt, d), lambda i, j, k: (i, j, 0)),
                  pl.BlockSpec((1, 6, d), mod_map)],
        out_specs=pl.BlockSpec((1, tt, d), lambda i, j, k: (i, j, 0)),
        out_shape=jax.ShapeDtypeStruct((b, l, d), F32),
        scratch_shapes=[pltpu.VMEM((et, tt), F32), pltpu.VMEM((et, tt), BF16),
                        pltpu.VMEM((d, tt), F32)],
        compiler_params=_params("parallel", "parallel", "arbitrary"),
        name="peer_mix",
    )(ht, n1, c1, b2, e2, u, vt, x, mod)


def _peer(ht, x1, mod, wq_t, keys, u, vt):
    tt = min(PEER_TOKENS, ht.shape[2])
    n1, c1, b2, e2 = _peer_topk(ht, wq_t, keys, tt)
    return _peer_dense(ht, n1, c1, b2, e2, u, vt, x1, mod, min(PEER_MIX_TOKENS, ht.shape[2]))


def _rope_tables(s):
    t = np.arange(s)
    pos = np.stack([t // GRID_W, t % GRID_W], axis=1).astype(np.float32)
    axis_dim = HEAD_DIM // 2
    freqs = 1.0 / (ROPE_THETA ** (jnp.arange(0, axis_dim, 2, dtype=F32) / axis_dim))
    ang = jnp.asarray(pos)[:, :, None] * freqs[None, None, :]
    cos, sin = jnp.cos(ang), jnp.sin(ang)
    zero = jnp.zeros_like(sin)
    def lanes(first_half, second_half):
        p = jnp.stack([first_half, second_half], axis=2).reshape(s, HEAD_DIM)
        return jnp.concatenate([p, p], axis=1)
    return lanes(cos, cos), lanes(-sin, zero), lanes(zero, sin)


def _prep_kernel(u_ref, v_ref, up_ref, vtp_ref):
    up_ref[...] = pltpu.bitcast(u_ref[0].astype(BF16), jnp.int32)
    vtp_ref[...] = pltpu.bitcast(v_ref[0].T.astype(BF16), jnp.int32)


def _prep_experts(peer_u, peer_v, layer):
    _, ne, d = peer_u.shape
    te = PEER_EXPERTS // 2
    return pl.pallas_call(
        _prep_kernel,
        grid=(ne // te,),
        in_specs=[pl.BlockSpec((1, te, d), lambda i: (layer, i, 0)),
                  pl.BlockSpec((1, te, d), lambda i: (layer, i, 0))],
        out_specs=[pl.BlockSpec((te // 2, d), lambda i: (i, 0)),
                   pl.BlockSpec((d // 2, te), lambda i: (0, i))],
        out_shape=[jax.ShapeDtypeStruct((ne // 2, d), jnp.int32),
                   jax.ShapeDtypeStruct((d // 2, ne), jnp.int32)],
        compiler_params=_params("parallel"),
        name="prep_experts",
    )(peer_u, peer_v)


def _block_diag_mean():
    i = np.arange(LANES)
    return jnp.asarray((i[:, None] // HEAD_DIM == i[None, :] // HEAD_DIM) / HEAD_DIM, BF16)


def kernel(x, c, ctx, c_ctx, ada_w, ada_b, norm_g, ab_w_in, ab_w_out, a_q_norm, a_k_norm, b_q_norm,
           b_k_norm, b_lambda, b_subln, c_w_in, c_w_out, c_q_norm, c_k_norm, c_rpb, peer_w_q,
           peer_keys, peer_u, peer_v):
    bsz, s, d = x.shape
    depth = ada_w.shape[0]
    assert depth == 2, "one attention-pair layer followed by one (final) neighbourhood layer"
    log2e = math.log2(math.e)
    scale = HEAD_DIM ** -0.5 * log2e

    cs = jnp.zeros((SUBLANES, d), F32).at[:bsz].set(c).at[bsz].set(c_ctx)
    mods = _ada(cs, ada_w, ada_b)
    bd = _block_diag_mean()
    rope = _rope_tables(s)

    for l in range(depth):
        last = l == depth - 1
        mod_lat = mods[l, :bsz].reshape(bsz, 6, d)
        mod_ctx = mods[l, bsz:bsz + 1].reshape(1, 6, d)
        g1 = norm_g[l, 0].reshape(1, d)
        g2 = norm_g[l, 1].reshape(1, d)
        i = l // 2
        wq_t = peer_w_q[l].T.astype(BF16)
        keys = peer_keys[l].reshape(2 * PEER_HEADS, PEER_N_KEYS, -1).astype(BF16)
        u, vt = _prep_experts(peer_u, peer_v, l)
        if l % 2 == 0:
            lam_init = 0.8 - 0.6 * math.exp(-0.3 * l)
            qa, ka, va, qb, kb, vb = jnp.split(
                ab_w_in[i], np.cumsum([512, 128, 128, 512, 512, 512])[:-1].tolist(), axis=1)
            w = jnp.concatenate([qa, qb, kb, ka, vb, va], axis=1).astype(BF16)
            nq, nk, nv = 1024, 640, 640
            gq = jnp.concatenate([jnp.tile(a_q_norm[i], A_HEADS), jnp.tile(b_q_norm[i], 2 * B_HEADS)])
            gk = jnp.concatenate([jnp.tile(b_k_norm[i], 2 * B_HEADS), jnp.tile(a_k_norm[i], A_KV_HEADS)])
            gq = (gq * scale).reshape(1, nq)
            gk = gk.reshape(1, nk)
            q_l, kt_l, v_l = _proj(x, mod_lat, g1, w, gq, gk, bd, rope, nq, nk, nv)
            q_c, kt_c, v_c = _proj(ctx, mod_ctx, g1, w, gq, gk, bd, None, nq, nk, nv)
            lam2 = b_lambda[i]
            sub = b_subln[i].reshape(1, LANES)
            kt_all = jnp.concatenate([kt_l, kt_c], axis=2)
            v_all = jnp.concatenate([v_l, v_c], axis=1)
            oa = _gqa(q_l, kt_all, v_all)
            ob = _diff(q_l, kt_all, v_all, lam2, sub, lam_init)
            w_out = ab_w_out[i].astype(BF16)
            x1, ht = _out(oa, 0, ob, 0, w_out, x, mod_lat, g2)
            if not last:
                oa_c = _gqa(q_c, kt_c, v_c)
                ob_c = _diff(q_c, kt_c, v_c, lam2, sub, lam_init)
                ctx1, ht_c = _out(oa_c, 0, ob_c, 0, w_out, ctx, mod_ctx, g2)
        else:
            w = c_w_in[i].astype(BF16)
            nq = nk = nv = C_HEADS * HEAD_DIM
            gq = (jnp.tile(c_q_norm[i], C_HEADS) * scale).reshape(1, nq)
            gk = jnp.tile(c_k_norm[i], C_HEADS).reshape(1, nk)
            q_l, kt_l, v_l = _proj(x, mod_lat, g1, w, gq, gk, bd, None, nq, nk, nv)
            q_c, kt_c, v_c = _proj(ctx, mod_ctx, g1, w, gq, gk, bd, None, nq, nk, nv)
            o = _na(q_l, kt_l, v_l, kt_c, v_c, _na_bias_table(c_rpb[i] * log2e))
            w_out = c_w_out[i].astype(BF16)
            x1, ht = _out(o, 0, o, 1, w_out, x, mod_lat, g2)
        x = _peer(ht, x1, mod_lat, wq_t, keys, u, vt)
        if not last:
            n_ctx = ctx1.shape[1]
            ht_all = ht_c.transpose(1, 0, 2).reshape(1, d // 2, bsz * n_ctx)
            ctx = _peer(ht_all, ctx1.reshape(1, bsz * n_ctx, d), mod_ctx, wq_t, keys, u, vt)
            ctx = ctx.reshape(bsz, n_ctx, d)
    return x
```

```python
import functools
import math

import numpy as np
import jax
import jax.numpy as jnp
from jax import lax
from jax.experimental import pallas as pl
from jax.experimental.pallas import tpu as pltpu

F32 = jnp.float32
BF16 = jnp.bfloat16

GRID_W = 64
HEAD_DIM = 64
ROPE_THETA = 10000.0
EPS = 1e-6
A_HEADS = 8
A_KV_HEADS = 2
B_HEADS = 4
C_HEADS = 16
NA_ROWS = 8
NA_COLS = 16
PEER_HEADS = 8
PEER_N_KEYS = 128
PEER_TOPK = 16

LANES = 128
SUBLANES = 8
BF16_ROWS = 16
VMEM_LIMIT = 56 * 1024 * 1024
NEG = -1e30

ADA_COLS = 1536
ROW_TILE = 512
Q_TILE = 256
GQA_SCORE_ELEMS = 3072 * 1024
DIFF_SCORE_ELEMS = 1536 * 1024
NA_BLOCK_ROWS = 4
PEER_TOKENS = 512
PEER_MIX_TOKENS = 1024
PEER_EXPERTS = 1024
PEER_GATE_LANES = LANES
_GELU_K = math.sqrt(2.0 / math.pi)
_LOG2E = math.log2(math.e)


def _params(*sem):
    return pltpu.CompilerParams(dimension_semantics=sem, vmem_limit_bytes=VMEM_LIMIT)


def _ada_kernel(cs_ref, w_ref, b_ref, o_ref):
    cs = cs_ref[...]
    a = cs * jax.nn.sigmoid(cs)
    o_ref[0] = jnp.dot(a, w_ref[0], preferred_element_type=F32,
                       precision=lax.Precision.HIGHEST) + b_ref[0]


def _ada(cs, ada_w, ada_b):
    depth, d, n = ada_w.shape
    tn = ADA_COLS
    return pl.pallas_call(
        _ada_kernel,
        grid=(depth, n // tn),
        in_specs=[pl.BlockSpec((SUBLANES, d), lambda l, j: (0, 0)),
                  pl.BlockSpec((1, d, tn), lambda l, j: (l, 0, j)),
                  pl.BlockSpec((1, 1, tn), lambda l, j: (l, 0, j))],
        out_specs=pl.BlockSpec((1, SUBLANES, tn), lambda l, j: (l, 0, j)),
        out_shape=jax.ShapeDtypeStruct((depth, SUBLANES, n), F32),
        compiler_params=_params("parallel", "parallel"),
        name="ada_mod",
    )(cs, ada_w, ada_b.reshape(depth, 1, n))


def _modulated_norm(x, gain, shift, scale):
    ms = jnp.mean(x * x, axis=-1, keepdims=True)
    return (x * lax.rsqrt(ms + EPS)) * gain * (1.0 + scale) + shift


def _proj_kernel(*refs, nq, nk, rope):
    if rope:
        (x_ref, mod_ref, g_ref, w_ref, gq_ref, gk_ref, bd_ref, cos_ref, sa_ref, sb_ref,
         q_ref, kt_ref, v_ref) = refs
    else:
        x_ref, mod_ref, g_ref, w_ref, gq_ref, gk_ref, bd_ref, q_ref, kt_ref, v_ref = refs
    h = _modulated_norm(x_ref[0], g_ref[...], mod_ref[0, 0:1, :], mod_ref[0, 1:2, :])
    p = jnp.dot(h.astype(BF16), w_ref[...], preferred_element_type=F32)
    bd = bd_ref[...]

    def head_norm(c0, gain):
        yc = p[:, c0:c0 + LANES]
        ss = jnp.dot((yc * yc).astype(BF16), bd, preferred_element_type=F32)
        yn = yc * lax.rsqrt(ss + EPS) * gain
        if rope:
            yn = (yn * cos_ref[...] + pltpu.roll(yn, LANES - 16, 1) * sa_ref[...]
                  + pltpu.roll(yn, 16, 1) * sb_ref[...])
        return yn

    for c in range(nq // LANES):
        c0 = c * LANES
        q_ref[0, :, c0:c0 + LANES] = head_norm(c0, gq_ref[:, c0:c0 + LANES]).astype(BF16)
    for c in range(nk // LANES):
        c0 = c * LANES
        yn = head_norm(nq + c0, gk_ref[:, c0:c0 + LANES])
        kt_ref[0, c0:c0 + LANES, :] = yn.T.astype(BF16)
    ones = jnp.ones((p.shape[0], LANES), BF16)
    for c in range((p.shape[1] - nq - nk) // LANES):
        c0 = nq + nk + c * LANES
        v_ref[0, :, 2 * c * LANES:(2 * c + 1) * LANES] = p[:, c0:c0 + LANES].astype(BF16)
        v_ref[0, :, (2 * c + 1) * LANES:(2 * c + 2) * LANES] = ones


def _proj(x, mod, gain, w, gq, gk, bd, rope_tabs, nq, nk, nv):
    b, l, d = x.shape
    tm = min(ROW_TILE, l)
    n = nq + nk + nv
    mod_map = (lambda i, j: (i, 0, 0)) if mod.shape[0] > 1 else (lambda i, j: (0, 0, 0))
    in_specs = [pl.BlockSpec((1, tm, d), lambda i, j: (i, j, 0)),
                pl.BlockSpec((1, 6, d), mod_map),
                pl.BlockSpec((1, d), lambda i, j: (0, 0)),
                pl.BlockSpec((d, n), lambda i, j: (0, 0)),
                pl.BlockSpec((1, nq), lambda i, j: (0, 0)),
                pl.BlockSpec((1, nk), lambda i, j: (0, 0)),
                pl.BlockSpec((LANES, LANES), lambda i, j: (0, 0))]
    args = [x, mod, gain, w, gq, gk, bd]
    if rope_tabs is not None:
        in_specs += [pl.BlockSpec((tm, LANES), lambda i, j: (j, 0))] * 3
        args += list(rope_tabs)
    return pl.pallas_call(
        functools.partial(_proj_kernel, nq=nq, nk=nk, rope=rope_tabs is not None),
        grid=(b, l // tm),
        in_specs=in_specs,
        out_specs=[pl.BlockSpec((1, tm, nq), lambda i, j: (i, j, 0)),
                   pl.BlockSpec((1, nk, tm), lambda i, j: (i, 0, j)),
                   pl.BlockSpec((1, tm, 2 * nv), lambda i, j: (i, j, 0))],
        out_shape=[jax.ShapeDtypeStruct((b, l, nq), BF16),
                   jax.ShapeDtypeStruct((b, nk, l), BF16),
                   jax.ShapeDtypeStruct((b, l, 2 * nv), BF16)],
        compiler_params=_params("parallel", "parallel"),
        name="in_proj",
    )(*args)


def _online_update(g, s, v, m_ref, acc_ref):
    m_prev = m_ref[g]
    m_new = jnp.maximum(m_prev, jnp.max(s, axis=-1, keepdims=True))
    alpha = jnp.exp2(m_prev - m_new)
    p = jnp.exp2(s - m_new)
    acc_ref[g] = alpha * acc_ref[g] + jnp.dot(p.astype(BF16), v, preferred_element_type=F32)
    m_ref[g] = m_new


def _normalised(acc_ref, g, rows):
    return acc_ref[g, rows, :LANES] / acc_ref[g, rows, LANES:LANES + 1]


def _attn_sweep(kt_rows, v_lane, qs_ref, kt_ref, v_ref, s_ref, m_ref, acc_ref):
    tq = qs_ref.shape[1] // len(kt_rows[0])
    m_ref[...] = jnp.full(m_ref.shape, -jnp.inf, F32)
    acc_ref[...] = jnp.zeros(acc_ref.shape, F32)

    def scores(g, kt):
        rows = kt_rows[g]
        if len(set(rows)) == 1:
            return jnp.dot(qs_ref[g], kt(rows[0]), preferred_element_type=F32)
        return jnp.concatenate(
            [jnp.dot(qs_ref[g, i * tq:(i + 1) * tq, :], kt(r), preferred_element_type=F32)
             for i, r in enumerate(rows)], axis=0)

    tk = s_ref.shape[2]
    n_chunks = kt_ref.shape[2] // tk

    def keys(c):
        if isinstance(c, int):
            return slice(c * tk, (c + 1) * tk)
        return pl.ds(pl.multiple_of(c * tk, LANES), tk)

    def kt_chunk(c):
        return lambda r: kt_ref[0, r:r + HEAD_DIM, keys(c)]

    def v_chunk(c, g):
        return v_ref[0, keys(c), v_lane[g]:v_lane[g] + 2 * LANES]

    s_ref[0] = scores(0, kt_chunk(0))

    def body(c, carry):
        s_ref[1] = scores(1, kt_chunk(c))
        _online_update(0, s_ref[0], v_chunk(c, 0), m_ref, acc_ref)
        s_ref[0] = scores(0, kt_chunk(c + 1))
        _online_update(1, s_ref[1], v_chunk(c, 1), m_ref, acc_ref)
        return carry

    lax.fori_loop(0, n_chunks - 1, body, 0)
    last = n_chunks - 1
    s_ref[1] = scores(1, kt_chunk(last))
    _online_update(0, s_ref[0], v_chunk(last, 0), m_ref, acc_ref)
    _online_update(1, s_ref[1], v_chunk(last, 1), m_ref, acc_ref)


def _gqa_kernel(q_ref, kt_ref, v_ref, o_ref, qs_ref, m_ref, acc_ref, s_ref):
    tq = q_ref.shape[1]
    group = A_HEADS // A_KV_HEADS
    for h in range(A_HEADS):
        qs_ref[h // group, (h % group) * tq:(h % group + 1) * tq, :] = (
            q_ref[0, :, h * HEAD_DIM:(h + 1) * HEAD_DIM])
    _attn_sweep([[g * HEAD_DIM] * group for g in range(A_KV_HEADS)], [0, 0],
                qs_ref, kt_ref, v_ref, s_ref, m_ref, acc_ref)
    lane = lax.broadcasted_iota(jnp.int32, (tq, LANES), 1)
    for j in range(A_HEADS // 2):
        g, i = (2 * j) // group, (2 * j) % group
        rows_a = slice(i * tq, (i + 1) * tq)
        rows_c = slice((i + 1) * tq, (i + 2) * tq)
        a = _normalised(acc_ref, g, rows_a)
        c = _normalised(acc_ref, g, rows_c)
        if g == 0:
            c = pltpu.roll(c, HEAD_DIM, 1)
        else:
            a = pltpu.roll(a, HEAD_DIM, 1)
        o_ref[0, :, j * LANES:(j + 1) * LANES] = jnp.where(lane < HEAD_DIM, a, c).astype(BF16)


def _diff_kernel(q_ref, kt_ref, v_ref, lam_ref, sub_ref, o_ref, qs_ref, m_ref, acc_ref, s_ref,
                 *, lam_init):
    tq = q_ref.shape[1]
    for g in range(2):
        for i in range(2):
            c0 = g * LANES + i * HEAD_DIM
            qs_ref[g, i * tq:(i + 1) * tq, :] = q_ref[0, :, c0:c0 + HEAD_DIM]
    _attn_sweep([[g * LANES, g * LANES + HEAD_DIM] for g in range(2)], [0, 2 * LANES],
                qs_ref, kt_ref, v_ref, s_ref, m_ref, acc_ref)
    lf = lam_ref[...]
    lam = (jnp.exp(jnp.sum(lf[0:1] * lf[1:2], axis=-1, keepdims=True))
           - jnp.exp(jnp.sum(lf[2:3] * lf[3:4], axis=-1, keepdims=True)) + lam_init)
    for g in range(2):
        o = (_normalised(acc_ref, g, slice(0, tq))
             - lam * _normalised(acc_ref, g, slice(tq, 2 * tq)))
        ms = jnp.mean(o * o, axis=-1, keepdims=True)
        o = (o * lax.rsqrt(ms + EPS)) * sub_ref[...] * (1.0 - lam_init)
        o_ref[0, :, g * LANES:(g + 1) * LANES] = o.astype(BF16)


def _attn_scratch(rows, tk):
    return [pltpu.VMEM((2, rows, HEAD_DIM), BF16), pltpu.VMEM((2, rows, 1), F32),
            pltpu.VMEM((2, rows, 2 * LANES), F32), pltpu.VMEM((2, rows, tk), F32)]


def _key_chunk(n_keys, rows, budget):
    cap = min(budget // rows, n_keys)
    return max(c for c in range(LANES, cap + 1, LANES) if n_keys % c == 0)


_KV_A_BLOCK = B_HEADS * 2 * HEAD_DIM // LANES


def _gqa(q, kt, v):
    b, lq, _ = q.shape
    tq = min(Q_TILE, lq)
    n_keys = kt.shape[2]
    blk = _KV_A_BLOCK
    return pl.pallas_call(
        _gqa_kernel,
        grid=(b, lq // tq),
        in_specs=[pl.BlockSpec((1, tq, A_HEADS * HEAD_DIM), lambda i, j: (i, j, 0)),
                  pl.BlockSpec((1, LANES, n_keys), lambda i, j: (i, blk, 0)),
                  pl.BlockSpec((1, n_keys, 2 * LANES), lambda i, j: (i, 0, blk))],
        out_specs=pl.BlockSpec((1, tq, A_HEADS * HEAD_DIM), lambda i, j: (i, j, 0)),
        out_shape=jax.ShapeDtypeStruct((b, lq, A_HEADS * HEAD_DIM), BF16),
        scratch_shapes=_attn_scratch(A_HEADS // A_KV_HEADS * tq,
                                     _key_chunk(n_keys, A_HEADS // A_KV_HEADS * tq, GQA_SCORE_ELEMS)),
        compiler_params=_params("parallel", "parallel"),
        name="gqa_attn",
    )(q, kt, v)


def _diff(q, kt, v, b_lambda, b_subln, lam_init):
    b, lq, _ = q.shape
    tq = min(2 * Q_TILE, lq)
    n_keys = kt.shape[2]
    w = 2 * LANES
    q0 = A_HEADS * HEAD_DIM // w
    return pl.pallas_call(
        functools.partial(_diff_kernel, lam_init=lam_init),
        grid=(b, B_HEADS // 2, lq // tq),
        in_specs=[pl.BlockSpec((1, tq, w), lambda i, h, j: (i, j, q0 + h)),
                  pl.BlockSpec((1, w, n_keys), lambda i, h, j: (i, h, 0)),
                  pl.BlockSpec((1, n_keys, 2 * w), lambda i, h, j: (i, 0, h)),
                  pl.BlockSpec((4, HEAD_DIM), lambda i, h, j: (0, 0)),
                  pl.BlockSpec((1, LANES), lambda i, h, j: (0, 0))],
        out_specs=pl.BlockSpec((1, tq, w), lambda i, h, j: (i, j, h)),
        out_shape=jax.ShapeDtypeStruct((b, lq, B_HEADS * LANES), BF16),
        scratch_shapes=_attn_scratch(2 * tq, _key_chunk(n_keys, 2 * tq, DIFF_SCORE_ELEMS)),
        compiler_params=_params("parallel", "parallel", "parallel"),
        name="diff_attn",
    )(q, kt, v, b_lambda, b_subln)


def _na_kernel(q_ref, km_ref, k0_ref, kp_ref, kc_ref, vm_ref, v0_ref, vp_ref, vc_ref, bias_ref,
               o_ref, *, rows):
    i = pl.program_id(1)
    tq = q_ref.shape[1]
    nb = 3 * tq
    qrow = NA_BLOCK_ROWS * i + lax.broadcasted_iota(jnp.int32, (tq, nb), 0) // GRID_W
    krow = NA_BLOCK_ROWS * (i - 1) + lax.broadcasted_iota(jnp.int32, (tq, nb), 1) // GRID_W
    rs = jnp.clip(qrow - NA_ROWS // 2, 0, rows - NA_ROWS)
    row_mask = jnp.where((krow >= rs) & (krow < rs + NA_ROWS), 0.0, NEG).astype(F32)
    lane = lax.broadcasted_iota(jnp.int32, (tq, LANES), 1)
    kts = (km_ref, k0_ref, kp_ref)
    vs = (vm_ref, v0_ref, vp_ref)
    for pair in range(C_HEADS // 2):
        outs = []
        for h in (2 * pair, 2 * pair + 1):
            r0 = h * HEAD_DIM
            q = q_ref[0, :, r0:r0 + HEAD_DIM]
            s_band = jnp.concatenate(
                [jnp.dot(q, k[0, r0:r0 + HEAD_DIM, :], preferred_element_type=F32) for k in kts],
                axis=1) + bias_ref[h] + row_mask
            s_ctx = jnp.dot(q, kc_ref[0, r0:r0 + HEAD_DIM, :], preferred_element_type=F32)
            m = jnp.maximum(jnp.max(s_band, axis=-1, keepdims=True),
                            jnp.max(s_ctx, axis=-1, keepdims=True))
            p_band = jnp.exp2(s_band - m)
            p_ctx = jnp.exp2(s_ctx - m)
            c0 = pair * 2 * LANES
            pv = jnp.dot(p_ctx.astype(BF16), vc_ref[0, :, c0:c0 + 2 * LANES],
                         preferred_element_type=F32)
            for t, v in enumerate(vs):
                pv += jnp.dot(p_band[:, t * tq:(t + 1) * tq].astype(BF16),
                              v[0, :, c0:c0 + 2 * LANES], preferred_element_type=F32)
            outs.append(pv[:, :LANES] / pv[:, LANES:LANES + 1])
        a, c = outs
        o_ref[0, :, pair * LANES:(pair + 1) * LANES] = jnp.where(lane < HEAD_DIM, a, c).astype(BF16)


def _na(q, kt_lat, v_lat, kt_ctx, v_ctx, bias):
    b, s, w = q.shape
    tq = NA_BLOCK_ROWS * GRID_W
    nblk = s // tq
    nc = kt_ctx.shape[2]
    rows = s // GRID_W

    def kt_spec(d):
        return pl.BlockSpec((1, w, tq), lambda i, j: (i, 0, jnp.clip(j + d, 0, nblk - 1)))

    wv = v_lat.shape[2]

    def v_spec(d):
        return pl.BlockSpec((1, tq, wv), lambda i, j: (i, jnp.clip(j + d, 0, nblk - 1), 0))

    return pl.pallas_call(
        functools.partial(_na_kernel, rows=rows),
        grid=(b, nblk),
        in_specs=[pl.BlockSpec((1, tq, w), lambda i, j: (i, j, 0)),
                  kt_spec(-1), kt_spec(0), kt_spec(1),
                  pl.BlockSpec((1, w, nc), lambda i, j: (i, 0, 0)),
                  v_spec(-1), v_spec(0), v_spec(1),
                  pl.BlockSpec((1, nc, wv), lambda i, j: (i, 0, 0)),
                  pl.BlockSpec(bias.shape, lambda i, j: (0, 0, 0))],
        out_specs=pl.BlockSpec((1, tq, w), lambda i, j: (i, j, 0)),
        out_shape=jax.ShapeDtypeStruct((b, s, w), BF16),
        compiler_params=_params("parallel", "parallel"),
        name="na_attn",
    )(q, kt_lat, kt_lat, kt_lat, kt_ctx, v_lat, v_lat, v_lat, v_ctx, bias)


def _na_bias_table(rpb):
    heads = rpb.shape[0]
    nr, nc = 2 * NA_ROWS - 1, 2 * NA_COLS - 1
    j = np.arange(GRID_W)
    dc = j[None, :] - j[:, None] + (NA_COLS - 1)
    cs = np.clip(j - NA_COLS // 2, 0, GRID_W - NA_COLS)
    col_ok = (j[None, :] >= cs[:, None]) & (j[None, :] < cs[:, None] + NA_COLS)
    onehot = (np.arange(nc)[:, None, None] == dc[None]) & col_ok[None]
    cols = jnp.einsum('hrm,mjk->hrjk', rpb, jnp.asarray(onehot, F32),
                      precision=lax.Precision.HIGHEST)
    cols = jnp.where(col_ok[None, None], cols, NEG)
    band = 3 * NA_BLOCK_ROWS
    per_q = [cols[:, NA_ROWS - 1 - NA_BLOCK_ROWS - qr:NA_ROWS - 1 - NA_BLOCK_ROWS - qr + band]
             for qr in range(NA_BLOCK_ROWS)]
    t = jnp.stack(per_q, axis=1)
    assert t.shape == (heads, NA_BLOCK_ROWS, band, GRID_W, GRID_W) and nr == band + NA_BLOCK_ROWS - 1
    return t.transpose(0, 1, 3, 2, 4).reshape(heads, NA_BLOCK_ROWS * GRID_W, band * GRID_W)


def _out_kernel(o1_ref, o2_ref, w_ref, x_ref, mod_ref, g_ref, x1_ref, ht_ref):
    half = o1_ref.shape[2]
    o = (jnp.dot(o1_ref[0], w_ref[0:half, :], preferred_element_type=F32)
         + jnp.dot(o2_ref[0], w_ref[half:, :], preferred_element_type=F32))
    x1 = x_ref[0] + mod_ref[0, 2:3, :] * o
    x1_ref[0] = x1
    h2 = _modulated_norm(x1, g_ref[...], mod_ref[0, 3:4, :], mod_ref[0, 4:5, :])
    ht_ref[0] = pltpu.bitcast(h2.T.astype(BF16), jnp.int32)


def _out(o1, o1_blk, o2, o2_blk, w, x, mod, gain):
    b, l, d = x.shape
    tm = min(ROW_TILE, l)
    half = d // 2
    mod_map = (lambda i, j: (i, 0, 0)) if mod.shape[0] > 1 else (lambda i, j: (0, 0, 0))
    return pl.pallas_call(
        _out_kernel,
        grid=(b, l // tm),
        in_specs=[pl.BlockSpec((1, tm, half), lambda i, j: (i, j, o1_blk)),
                  pl.BlockSpec((1, tm, half), lambda i, j: (i, j, o2_blk)),
                  pl.BlockSpec((d, d), lambda i, j: (0, 0)),
                  pl.BlockSpec((1, tm, d), lambda i, j: (i, j, 0)),
                  pl.BlockSpec((1, 6, d), mod_map),
                  pl.BlockSpec((1, d), lambda i, j: (0, 0))],
        out_specs=[pl.BlockSpec((1, tm, d), lambda i, j: (i, j, 0)),
                   pl.BlockSpec((1, d // 2, tm), lambda i, j: (i, 0, j))],
        out_shape=[jax.ShapeDtypeStruct((b, l, d), F32),
                   jax.ShapeDtypeStruct((b, d // 2, l), jnp.int32)],
        compiler_params=_params("parallel", "parallel"),
        name="out_proj",
    )(o1, o2, w, x, mod, gain)


def _top16(s, iota, exact):
    rank = jnp.full(s.shape, float(PEER_TOPK), F32)
    big = float(s.shape[0])
    vals = []
    for r in range(PEER_TOPK):
        m = jnp.max(s, axis=0, keepdims=True)
        hit = s == m
        if exact:
            hit = iota == jnp.min(jnp.where(hit, iota, big), axis=0, keepdims=True)
        rank = jnp.where(hit, float(r), rank)
        s = jnp.where(hit, -jnp.inf, s)
        vals.append(m)
    return vals, rank


def _sorting_network(n):
    pairs = []
    p = 1
    while p < n:
        k = p
        while k >= 1:
            for j in range(k % p, n - k, 2 * k):
                for i in range(min(k, n - j - k)):
                    if (i + j) // (2 * p) == (i + j + k) // (2 * p):
                        pairs.append((i + j, i + j + k))
            k //= 2
        p *= 2
    return pairs


def _top16_values(s):
    n_tiles = s.shape[0] // SUBLANES
    w = [s[SUBLANES * j:SUBLANES * (j + 1)] for j in range(n_tiles)]
    for i, j in _sorting_network(PEER_TOPK):
        if j < n_tiles:
            w[i], w[j] = jnp.maximum(w[i], w[j]), jnp.minimum(w[i], w[j])
    vals = []
    for r in range(PEER_TOPK):
        m = jnp.max(w[0], axis=0, keepdims=True)
        vals.append(m)
        hit = w[0] == m
        for j in range(min(n_tiles, PEER_TOPK - 1 - r)):
            w[j] = jnp.where(hit, w[j + 1] if j + 1 < n_tiles else -jnp.inf, w[j])
    return vals


def _rows_tile(rows, iota8):
    t = jnp.zeros((SUBLANES,) + rows[0].shape[1:], F32)
    for k, r in enumerate(rows):
        t = jnp.where(iota8 == k, r, t)
    return t


_CAND_COUNTS = [PEER_TOPK // (a + 1) for a in range(SUBLANES)]


def _retrieve(s1, s2, exact):
    iota = lax.broadcasted_iota(jnp.int32, (PEER_N_KEYS, LANES), 0).astype(F32)
    iota8 = lax.broadcasted_iota(jnp.int32, (SUBLANES, LANES), 0)
    iota_c = lax.broadcasted_iota(jnp.int32, (SUBLANES * (SUBLANES + 2), LANES), 0).astype(F32)
    top = float(PEER_TOPK)
    if exact:
        v1, r1 = _top16(s1, iota, True)
        v2, r2 = _top16(s2, iota, True)
        is_rank1 = [r1 == float(a) for a in range(PEER_TOPK)]
        in_top1 = r1 < top
        repeats = 0.0
    else:
        v1 = _top16_values(s1)
        v2 = _top16_values(s2)
        is_rank1 = [s1 == v for v in v1]
        in_top1 = s1 >= v1[-1]
        repeats = sum((a == b).astype(F32) for v in (v1, v2) for a, b in zip(v[:-1], v[1:]))
        r2 = jnp.full(s2.shape, top, F32)
        for r in reversed(range(PEER_TOPK)):
            r2 = jnp.where(s2 >= v2[r], float(r), r2)
    e1 = [jnp.exp(v - v1[0]) for v in v1]
    e2 = [jnp.exp(v - v2[0]) for v in v2]
    v2_lo, v2_hi = _rows_tile(v2[:8], iota8), _rows_tile(v2[8:], iota8)
    e2_lo, e2_hi = _rows_tile(e2[:8], iota8), _rows_tile(e2[8:], iota8)
    v1_hi, e1_hi = _rows_tile(v1[8:], iota8), _rows_tile(e1[8:], iota8)
    cand, ecand = [], []
    for a, cnt in enumerate(_CAND_COUNTS):
        cand.append(jnp.where(iota8 < cnt, v1[a] + v2_lo, -jnp.inf))
        ecand.append(e1[a] * e2_lo)
        if a == 0:
            cand.append(v1[0] + v2_hi)
            ecand.append(e1[0] * e2_hi)
    cand.append(v1_hi + v2[0])
    ecand.append(e1_hi * e2[0])
    cand = jnp.concatenate(cand, axis=0)
    ecand = jnp.concatenate(ecand, axis=0)
    if exact:
        _, rc = _top16(cand, iota_c, True)
        sel = (rc < top).astype(F32)
    else:
        vc = _top16_values(cand)
        sel = (cand >= vc[-1]).astype(F32)
        repeats = repeats + sum((a == b).astype(F32) for a, b in zip(vc[:-1], vc[1:]))
    ranked = (jnp.sum(sel, axis=0, keepdims=True)
              + jnp.sum(in_top1.astype(F32), axis=0, keepdims=True)
              + jnp.sum((r2 < top).astype(F32), axis=0, keepdims=True) + repeats)
    z = jnp.sum(sel * ecand, axis=0, keepdims=True)
    n_rows = [jnp.sum(sel[0:2 * SUBLANES], axis=0, keepdims=True)]
    for a in range(1, SUBLANES):
        n_rows.append(jnp.sum(sel[SUBLANES * (a + 1):SUBLANES * (a + 2)], axis=0, keepdims=True))
    n_hi = sel[SUBLANES * (SUBLANES + 1):]
    n1 = jnp.zeros(s1.shape, F32)
    for a in range(PEER_TOPK):
        na = n_rows[a] if a < SUBLANES else n_hi[a - SUBLANES:a - SUBLANES + 1]
        n1 = jnp.where(is_rank1[a], na, n1)
    return n1, jnp.exp(s1 - v1[0]) / z, r2, jnp.exp(s2 - v2[0]), ranked


def _topk_kernel(ht_ref, wq_ref, keys_ref, n1_ref, c1_ref, b2_ref, e2_ref, qt_ref, s_ref):
    chunks = ht_ref.shape[2] // LANES
    qt_ref[...] = jnp.dot(wq_ref[...], pltpu.bitcast(ht_ref[0], BF16),
                          preferred_element_type=F32).astype(BF16)
    for hp in range(2 * PEER_HEADS):
        s_ref[hp] = jnp.dot(keys_ref[hp], qt_ref[hp * PEER_N_KEYS:(hp + 1) * PEER_N_KEYS, :],
                            preferred_element_type=F32)

    def emit(h, sl, exact):
        n1, c1, b2, e2, ranked = _retrieve(s_ref[2 * h, :, sl], s_ref[2 * h + 1, :, sl], exact)
        n1_ref[0, h, :, sl] = n1
        c1_ref[0, h, :, sl] = c1
        b2_ref[0, h, :, sl] = pltpu.bitcast(b2.astype(BF16), jnp.int32)
        e2_ref[0, h, :, sl] = pltpu.bitcast(e2.astype(BF16), jnp.int32)
        return ranked

    def pair(i, carry):
        h = i // (chunks // 2)
        c0 = (i % (chunks // 2)) * 2 * LANES
        sls = [pl.ds(pl.multiple_of(c0 + j * LANES, LANES), LANES) for j in range(2)]
        ranked = jnp.maximum(emit(h, sls[0], False), emit(h, sls[1], False))

        @pl.when(jnp.max(ranked) > float(3 * PEER_TOPK))
        def _():
            for sl in sls:
                emit(h, sl, True)

        return carry

    lax.fori_loop(0, PEER_HEADS * chunks // 2, pair, 0)


def _peer_topk(ht, wq_t, keys, tt):
    b, d2, l = ht.shape
    nqd = wq_t.shape[0]
    shp = (b, PEER_HEADS, PEER_N_KEYS, l)
    shp_packed = (b, PEER_HEADS, PEER_N_KEYS // 2, l)
    ospec = pl.BlockSpec((1, PEER_HEADS, PEER_N_KEYS, tt), lambda i, j: (i, 0, 0, j))
    ospec_packed = pl.BlockSpec((1, PEER_HEADS, PEER_N_KEYS // 2, tt), lambda i, j: (i, 0, 0, j))
    return pl.pallas_call(
        _topk_kernel,
        grid=(b, l // tt),
        in_specs=[pl.BlockSpec((1, d2, tt), lambda i, j: (i, 0, j)),
                  pl.BlockSpec((nqd, 2 * d2), lambda i, j: (0, 0)),
                  pl.BlockSpec(keys.shape, lambda i, j: (0, 0, 0))],
        out_specs=[ospec, ospec, ospec_packed, ospec_packed],
        out_shape=[jax.ShapeDtypeStruct(shp, F32), jax.ShapeDtypeStruct(shp, F32),
                   jax.ShapeDtypeStruct(shp_packed, jnp.int32),
                   jax.ShapeDtypeStruct(shp_packed, jnp.int32)],
        scratch_shapes=[pltpu.VMEM((nqd, tt), BF16),
                        pltpu.VMEM((2 * PEER_HEADS, PEER_N_KEYS, tt), F32)],
        compiler_params=_params("parallel", "parallel"),
        name="peer_topk",
    )(ht, wq_t, keys)


def _peer_kernel(ht_ref, n1_ref, c1_ref, b2_ref, e2_ref, u_ref, vt_ref, x_ref, mod_ref, o_ref,
                 a_ref, g_ref, acc_ref):
    e = pl.program_id(2)
    et = 2 * u_ref.shape[0]
    ht = pltpu.bitcast(ht_ref[0], BF16)

    @pl.when(e == 0)
    def _():
        acc_ref[...] = jnp.zeros(acc_ref.shape, F32)

    a_ref[...] = jnp.dot(pltpu.bitcast(u_ref[...], BF16), ht, preferred_element_type=F32)
    tile3 = (PEER_N_KEYS // BF16_ROWS, BF16_ROWS, PEER_GATE_LANES)
    for k in range(et // PEER_N_KEYS):
        rows = slice(k * PEER_N_KEYS, (k + 1) * PEER_N_KEYS)
        for c in range(ht.shape[1] // PEER_GATE_LANES):
            sl = slice(c * PEER_GATE_LANES, (c + 1) * PEER_GATE_LANES)
            w = jnp.zeros(tile3, BF16)
            for h in range(PEER_HEADS):
                n_t = jnp.broadcast_to(n1_ref[0, h, k:k + 1, sl], tile3[1:]).astype(BF16)[None]
                c_t = jnp.broadcast_to(c1_ref[0, h, k:k + 1, sl], tile3[1:]).astype(BF16)[None]
                b2 = pltpu.bitcast(b2_ref[0, h, :, sl], BF16).reshape(tile3)
                e2 = pltpu.bitcast(e2_ref[0, h, :, sl], BF16).reshape(tile3)
                w = w + jnp.clip(n_t - b2, 0, 1) * (e2 * c_t)
            a = a_ref[rows, sl].astype(BF16).reshape(tile3)
            z = a * (a * a * (-2.0 * _GELU_K * 0.044715 * _LOG2E) - 2.0 * _GELU_K * _LOG2E)
            act = a / (1.0 + jnp.exp2(z))
            g_ref[rows, sl] = (w * act).reshape(PEER_N_KEYS, PEER_GATE_LANES)
    acc_ref[...] += jnp.dot(pltpu.bitcast(vt_ref[...], BF16), g_ref[...],
                            preferred_element_type=F32)

    @pl.when(e == pl.num_programs(2) - 1)
    def _():
        o_ref[0] = x_ref[0] + mod_ref[0, 5:6, :] * acc_ref[...].T


def _peer_dense(ht, n1, c1, b2, e2, u, vt, x, mod, tt):
    b, d2, l = ht.shape
    d = 2 * d2
    ne = 2 * u.shape[0]
    et = PEER_EXPERTS
    mod_map = (lambda i, j, k: (i, 0, 0)) if mod.shape[0] > 1 else (lambda i, j, k: (0, 0, 0))
    sel_spec = pl.BlockSpec((1, PEER_HEADS, PEER_N_KEYS // 2, tt), lambda i, j, k: (i, 0, 0, j))
    row_spec = pl.BlockSpec((1, PEER_HEADS, et // PEER_N_KEYS, tt), lambda i, j, k: (i, 0, k, j))
    return pl.pallas_call(
        _peer_kernel,
        grid=(b, l // tt, ne // et),
        in_specs=[pl.BlockSpec((1, d2, tt), lambda i, j, k: (i, 0, j)),
                  row_spec, row_spec, sel_spec, sel_spec,
                  pl.BlockSpec((et // 2, d), lambda i, j, k: (k, 0)),
                  pl.BlockSpec((d2, et), lambda i, j, k: (0, k)),
                  pl.BlockSpec((1, tt, d), lambda i, j, k: (i, j, 0)),
                  pl.BlockSpec((1, 6, d), mod_map)],
        out_specs=pl.BlockSpec((1, tt, d), lambda i, j, k: (i, j, 0)),
        out_shape=jax.ShapeDtypeStruct((b, l, d), F32),
        scratch_shapes=[pltpu.VMEM((et, tt), F32), pltpu.VMEM((et, tt), BF16),
                        pltpu.VMEM((d, tt), F32)],
        compiler_params=_params("parallel", "parallel", "arbitrary"),
        name="peer_mix",
    )(ht, n1, c1, b2, e2, u, vt, x, mod)


def _peer(ht, x1, mod, wq_t, keys, u, vt):
    tt = min(PEER_TOKENS, ht.shape[2])
    n1, c1, b2, e2 = _peer_topk(ht, wq_t, keys, tt)
    return _peer_dense(ht, n1, c1, b2, e2, u, vt, x1, mod, min(PEER_MIX_TOKENS, ht.shape[2]))


def _rope_tables(s):
    t = np.arange(s)
    pos = np.stack([t // GRID_W, t % GRID_W], axis=1).astype(np.float32)
    axis_dim = HEAD_DIM // 2
    freqs = 1.0 / (ROPE_THETA ** (jnp.arange(0, axis_dim, 2, dtype=F32) / axis_dim))
    ang = jnp.asarray(pos)[:, :, None] * freqs[None, None, :]
    cos, sin = jnp.cos(ang), jnp.sin(ang)
    zero = jnp.zeros_like(sin)
    def lanes(first_half, second_half):
        p = jnp.stack([first_half, second_half], axis=2).reshape(s, HEAD_DIM)
        return jnp.concatenate([p, p], axis=1)
    return lanes(cos, cos), lanes(-sin, zero), lanes(zero, sin)


def _prep_kernel(u_ref, v_ref, up_ref, vtp_ref):
    up_ref[...] = pltpu.bitcast(u_ref[0].astype(BF16), jnp.int32)
    vtp_ref[...] = pltpu.bitcast(v_ref[0].T.astype(BF16), jnp.int32)


def _prep_experts(peer_u, peer_v, layer):
    _, ne, d = peer_u.shape
    te = PEER_EXPERTS // 2
    return pl.pallas_call(
        _prep_kernel,
        grid=(ne // te,),
        in_specs=[pl.BlockSpec((1, te, d), lambda i: (layer, i, 0)),
                  pl.BlockSpec((1, te, d), lambda i: (layer, i, 0))],
        out_specs=[pl.BlockSpec((te // 2, d), lambda i: (i, 0)),
                   pl.BlockSpec((d // 2, te), lambda i: (0, i))],
        out_shape=[jax.ShapeDtypeStruct((ne // 2, d), jnp.int32),
                   jax.ShapeDtypeStruct((d // 2, ne), jnp.int32)],
        compiler_params=_params("parallel"),
        name="prep_experts",
    )(peer_u, peer_v)


def _block_diag_mean():
    i = np.arange(LANES)
    return jnp.asarray((i[:, None] // HEAD_DIM == i[None, :] // HEAD_DIM) / HEAD_DIM, BF16)


def kernel(x, c, ctx, c_ctx, ada_w, ada_b, norm_g, ab_w_in, ab_w_out, a_q_norm, a_k_norm, b_q_norm,
           b_k_norm, b_lambda, b_subln, c_w_in, c_w_out, c_q_norm, c_k_norm, c_rpb, peer_w_q,
           peer_keys, peer_u, peer_v):
    bsz, s, d = x.shape
    depth = ada_w.shape[0]
    assert depth == 2, "one attention-pair layer followed by one (final) neighbourhood layer"
    log2e = math.log2(math.e)
    scale = HEAD_DIM ** -0.5 * log2e

    cs = jnp.zeros((SUBLANES, d), F32).at[:bsz].set(c).at[bsz].set(c_ctx)
    mods = _ada(cs, ada_w, ada_b)
    bd = _block_diag_mean()
    rope = _rope_tables(s)

    for l in range(depth):
        last = l == depth - 1
        mod_lat = mods[l, :bsz].reshape(bsz, 6, d)
        mod_ctx = mods[l, bsz:bsz + 1].reshape(1, 6, d)
        g1 = norm_g[l, 0].reshape(1, d)
        g2 = norm_g[l, 1].reshape(1, d)
        i = l // 2
        wq_t = peer_w_q[l].T.astype(BF16)
        keys = peer_keys[l].reshape(2 * PEER_HEADS, PEER_N_KEYS, -1).astype(BF16)
        u, vt = _prep_experts(peer_u, peer_v, l)
        if l % 2 == 0:
            lam_init = 0.8 - 0.6 * math.exp(-0.3 * l)
            qa, ka, va, qb, kb, vb = jnp.split(
                ab_w_in[i], np.cumsum([512, 128, 128, 512, 512, 512])[:-1].tolist(), axis=1)
            w = jnp.concatenate([qa, qb, kb, ka, vb, va], axis=1).astype(BF16)
            nq, nk, nv = 1024, 640, 640
            gq = jnp.concatenate([jnp.tile(a_q_norm[i], A_HEADS), jnp.tile(b_q_norm[i], 2 * B_HEADS)])
            gk = jnp.concatenate([jnp.tile(b_k_norm[i], 2 * B_HEADS), jnp.tile(a_k_norm[i], A_KV_HEADS)])
            gq = (gq * scale).reshape(1, nq)
            gk = gk.reshape(1, nk)
            q_l, kt_l, v_l = _proj(x, mod_lat, g1, w, gq, gk, bd, rope, nq, nk, nv)
            q_c, kt_c, v_c = _proj(ctx, mod_ctx, g1, w, gq, gk, bd, None, nq, nk, nv)
            lam2 = b_lambda[i]
            sub = b_subln[i].reshape(1, LANES)
            kt_all = jnp.concatenate([kt_l, kt_c], axis=2)
            v_all = jnp.concatenate([v_l, v_c], axis=1)
            oa = _gqa(q_l, kt_all, v_all)
            ob = _diff(q_l, kt_all, v_all, lam2, sub, lam_init)
            w_out = ab_w_out[i].astype(BF16)
            x1, ht = _out(oa, 0, ob, 0, w_out, x, mod_lat, g2)
            if not last:
                oa_c = _gqa(q_c, kt_c, v_c)
                ob_c = _diff(q_c, kt_c, v_c, lam2, sub, lam_init)
                ctx1, ht_c = _out(oa_c, 0, ob_c, 0, w_out, ctx, mod_ctx, g2)
        else:
            w = c_w_in[i].astype(BF16)
            nq = nk = nv = C_HEADS * HEAD_DIM
            gq = (jnp.tile(c_q_norm[i], C_HEADS) * scale).reshape(1, nq)
            gk = jnp.tile(c_k_norm[i], C_HEADS).reshape(1, nk)
            q_l, kt_l, v_l = _proj(x, mod_lat, g1, w, gq, gk, bd, None, nq, nk, nv)
            q_c, kt_c, v_c = _proj(ctx, mod_ctx, g1, w, gq, gk, bd, None, nq, nk, nv)
            o = _na(q_l, kt_l, v_l, kt_c, v_c, _na_bias_table(c_rpb[i] * log2e))
            w_out = c_w_out[i].astype(BF16)
            x1, ht = _out(o, 0, o, 1, w_out, x, mod_lat, g2)
        x = _peer(ht, x1, mod_lat, wq_t, keys, u, vt)
        if not last:
            n_ctx = ctx1.shape[1]
            ht_all = ht_c.transpose(1, 0, 2).reshape(1, d // 2, bsz * n_ctx)
            ctx = _peer(ht_all, ctx1.reshape(1, bsz * n_ctx, d), mod_ctx, wq_t, keys, u, vt)
            ctx = ctx.reshape(bsz, n_ctx, d)
    return x
```
